```python
import jax, jax.numpy as jnp
from jax import lax
import numpy as np

D_MODEL = 1024
BATCH = 2
SEQ = 16384
DEPTH = 2

CHUNK = 64
N_MIXERS = 4
HEAD_DIM = 64
D_MIX = D_MODEL
D_BRANCH = D_MIX // N_MIXERS
N_HEADS = D_BRANCH // HEAD_DIM
LOOKBACK_CHUNKS = 8
BAND = (LOOKBACK_CHUNKS + 1) * CHUNK
MAX_REL = 128
SG_CHUNK = 128
Q_BLOCK = 128
EPS = 1e-6
IN_SIZES = ([D_BRANCH] * 4
            + [D_BRANCH] * 3
            + [D_BRANCH] * 4
            + [N_HEADS]
            + [D_BRANCH] * 4)
N_IN = sum(IN_SIZES)

kernel_name = "hybrid_chunk_stream_heads"


def rms_norm(x, g):
    xf = x.astype(jnp.float32)
    y = xf * lax.rsqrt(jnp.mean(xf * xf, axis=-1, keepdims=True) + EPS)
    return (y * g.astype(jnp.float32)).astype(x.dtype)


def layer_norm(x, g):
    xf = x.astype(jnp.float32)
    mu = jnp.mean(xf, axis=-1, keepdims=True)
    xc = xf - mu
    y = xc * lax.rsqrt(jnp.mean(xc * xc, axis=-1, keepdims=True) + EPS)
    return (y * g.astype(jnp.float32)).astype(x.dtype)


def heads(t):
    b, s, _ = t.shape
    return t.reshape(b, s, N_HEADS, HEAD_DIM)


def chunk_band(t):
    b, s, h, dh = t.shape
    nc = s // CHUNK
    tp = jnp.pad(t, ((0, 0), (LOOKBACK_CHUNKS * CHUNK, 0), (0, 0), (0, 0)))
    tc = tp.reshape(b, nc + LOOKBACK_CHUNKS, CHUNK, h, dh)
    return jnp.concatenate([tc[:, m:m + nc] for m in range(LOOKBACK_CHUNKS + 1)], axis=2)


def chunk_relbias_attention(q, k, v, rel_bias):
    b, s, h, dh = q.shape
    nc = s // CHUNK
    qc = q.reshape(b, nc, CHUNK, h, dh)
    kb = chunk_band(k)
    vb = chunk_band(v)
    i = np.arange(CHUNK)[:, None]
    j = np.arange(BAND)[None, :]
    rel = np.clip(i - j + LOOKBACK_CHUNKS * CHUNK, -MAX_REL, MAX_REL) + MAX_REL
    bias = rel_bias[:, rel].astype(jnp.float32)
    key_chunk = jnp.arange(nc)[:, None] - LOOKBACK_CHUNKS + jnp.arange(BAND)[None, :] // CHUNK
    valid = key_chunk >= 0
    sc = jnp.einsum('bnihd,bnjhd->bnhij', qc, kb).astype(jnp.float32) * (dh ** -0.5)
    sc = sc + bias[None, None]
    sc = jnp.where(valid[None, :, None, None, :], sc, -jnp.inf)
    p = jax.nn.softmax(sc, axis=-1).astype(v.dtype)
    out = jnp.einsum('bnhij,bnjhd->bnihd', p, vb)
    return out.reshape(b, s, h * dh)


def spatial_gating(u, v, v_gain, w_s, b_s):
    b, s, _ = u.shape
    c = D_BRANCH // N_HEADS
    vn = layer_norm(v, v_gain).reshape(b, s // SG_CHUNK, SG_CHUNK, N_HEADS, c)
    w = w_s * jnp.tril(jnp.ones((SG_CHUNK, SG_CHUNK), w_s.dtype))[None]
    mixed = jnp.einsum('gts,bnsgc->bntgc', w, vn) + jnp.transpose(b_s)[None, None, :, :, None]
    return u * mixed.reshape(b, s, D_BRANCH)


def query_blocks(t):
    b, s = t.shape[:2]
    return jnp.moveaxis(t.reshape((b, s // Q_BLOCK, Q_BLOCK) + t.shape[2:]), 1, 0)


def forgetting_attention(q, k, v, f_logit):
    b, s, h, dh = q.shape
    nb = s // Q_BLOCK
    c = jnp.cumsum(jax.nn.log_sigmoid(f_logit.astype(jnp.float32)), axis=1)
    c_k = jnp.transpose(c, (0, 2, 1))[:, :, None, :]
    k_pos = jnp.arange(s)
    scale = dh ** -0.5

    def block(args):
        q_i, c_i, s0 = args
        sc = jnp.einsum('bqhd,bkhd->bhqk', q_i, k).astype(jnp.float32) * scale
        sc = sc + jnp.transpose(c_i, (0, 2, 1))[..., None] - c_k
        q_pos = s0 + jnp.arange(Q_BLOCK)
        mask = k_pos[None, :] <= q_pos[:, None]
        sc = jnp.where(mask, sc, -jnp.inf)
        p = jax.nn.softmax(sc, axis=-1).astype(v.dtype)
        return jnp.einsum('bhqk,bkhd->bqhd', p, v)

    out = lax.map(block, (query_blocks(q), query_blocks(c), jnp.arange(nb) * Q_BLOCK))
    return jnp.moveaxis(out, 0, 1).reshape(b, s, h * dh)


def stick_breaking_attention(q, k, v):
    b, s, h, dh = q.shape
    nb = s // Q_BLOCK
    k_pos = jnp.arange(s)
    scale = dh ** -0.5

    def block(args):
        q_i, s0 = args
        z = jnp.einsum('bqhd,bkhd->bhqk', q_i, k).astype(jnp.float32) * scale
        q_pos = s0 + jnp.arange(Q_BLOCK)
        mask = k_pos[None, :] < q_pos[:, None]
        log_1m = jnp.where(mask, jax.nn.log_sigmoid(-z), 0.0)
        between = lax.cumsum(log_1m, axis=3, reverse=True) - log_1m
        a = jnp.where(mask, jnp.exp(jax.nn.log_sigmoid(z) + between), 0.0)
        return jnp.einsum('bhqk,bkhd->bqhd', a.astype(v.dtype), v)

    out = lax.map(block, (query_blocks(q), jnp.arange(nb) * Q_BLOCK))
    return jnp.moveaxis(out, 0, 1).reshape(b, s, h * dh)


def hybrid_layer(x, norm_g, w_in, b_f, rel_bias, w_s, b_s, v_gain, branch_gain, w_out):
    h = rms_norm(x, norm_g)
    p = jnp.einsum('bsd,dn->bsn', h, w_in)
    (qa, ka, va, ga,
     ub, vbr, gb,
     qc, kc, vc, gc, fc,
     qd, kd, vd, gd) = jnp.split(p, [int(o) for o in np.cumsum(IN_SIZES)[:-1]], axis=-1)
    y_a = chunk_relbias_attention(heads(qa), heads(ka), heads(va), rel_bias)
    y_b = spatial_gating(ub, vbr, v_gain, w_s, b_s)
    y_c = forgetting_attention(heads(qc), heads(kc), heads(vc), fc + b_f)
    y_d = stick_breaking_attention(heads(qd), heads(kd), heads(vd))
    merged = jnp.concatenate([
        rms_norm(y_a, branch_gain[0]) * jax.nn.silu(ga),
        rms_norm(y_b, branch_gain[1]) * jax.nn.silu(gb),
        rms_norm(y_c, branch_gain[2]) * jax.nn.silu(gc),
        rms_norm(y_d, branch_gain[3]) * jax.nn.silu(gd),
    ], axis=-1)
    return x + jnp.einsum('bsm,md->bsd', merged, w_out)


def setup_inputs(seed: int = 0) -> dict:
    key = jax.random.key(seed)
    ks = jax.random.split(key, 12)
    f32 = jnp.float32
    x = jax.random.normal(ks[0], (BATCH, SEQ, D_MODEL), f32)
    norm_g = 1.0 + 0.02 * jax.random.normal(ks[1], (DEPTH, D_MODEL), f32)
    w_in = jax.random.normal(ks[2], (DEPTH, D_MODEL, N_IN), f32) * D_MODEL ** -0.5
    b_f = 4.0 + 0.5 * jax.random.normal(ks[3], (DEPTH, N_HEADS), f32)
    rel_bias = 0.5 * jax.random.normal(ks[4], (DEPTH, N_HEADS, 2 * MAX_REL + 1), f32)
    w_s = jax.random.normal(ks[5], (DEPTH, N_HEADS, SG_CHUNK, SG_CHUNK), f32) * SG_CHUNK ** -0.5
    b_s = 1.0 + 0.1 * jax.random.normal(ks[6], (DEPTH, N_HEADS, SG_CHUNK), f32)
    v_gain = 1.0 + 0.02 * jax.random.normal(ks[7], (DEPTH, D_BRANCH), f32)
    branch_gain = 1.0 + 0.02 * jax.random.normal(ks[8], (DEPTH, N_MIXERS, D_BRANCH), f32)
    w_out = jax.random.normal(ks[9], (DEPTH, D_MIX, D_MODEL), f32) * (0.5 * D_MIX ** -0.5)
    final_g = 1.0 + 0.02 * jax.random.normal(ks[10], (D_MODEL,), f32)
    return {"x": x, "norm_g": norm_g, "w_in": w_in, "b_f": b_f, "rel_bias": rel_bias,
            "w_s": w_s, "b_s": b_s, "v_gain": v_gain, "branch_gain": branch_gain,
            "w_out": w_out, "final_g": final_g}


def reference(x, norm_g, w_in, b_f, rel_bias, w_s, b_s, v_gain, branch_gain, w_out, final_g):
    for l in range(DEPTH):
        x = hybrid_layer(x, norm_g[l], w_in[l], b_f[l], rel_bias[l], w_s[l], b_s[l],
                         v_gain[l], branch_gain[l], w_out[l])
    return rms_norm(x, final_g)
```

```python
import functools

import jax
import jax.numpy as jnp
import numpy as np
from jax import lax
from jax.experimental import pallas as pl
from jax.experimental.pallas import tpu as pltpu

D_MODEL = 1024
N_HEADS = 4
HEAD_DIM = 64
D_BRANCH = N_HEADS * HEAD_DIM
CHUNK = 64
LOOKBACK_CHUNKS = 8
MAX_REL = 128
SG_CHUNK = 128
EPS = 1e-6
SCALE = HEAD_DIM ** -0.5
HEAD_PAD = 128
D_PAD = N_HEADS * HEAD_PAD
NEG = -1e30

F32 = jnp.float32
BF16 = jnp.bfloat16

TM_IN = 256
TQ_A = 256
LOOK_A = LOOKBACK_CHUNKS * CHUNK
TQ_C = 512
TQ_D = 512
TK_D = 256
TM_OUT = 256
VMEM_LIMIT = 56 * 1024 * 1024


def _nt_dot(a, b):
    return lax.dot_general(a, b, (((1,), (1,)), ((), ())), preferred_element_type=F32)


def _dot(a, b):
    return jnp.dot(a, b, preferred_element_type=F32)


def _inproj_kernel(x_ref, g_ref, wa_ref, wb_ref, wc_ref, wf_ref, wd_ref, bf_ref,
                   pa_ref, pb_ref, qc_ref, kc_ref, vc_ref, gc_ref, c_ref,
                   qd_ref, kd_ref, vd_ref, gd_ref, carry_ref):
    i = pl.program_id(1)
    tm = x_ref.shape[0]
    x = x_ref[...]
    ms = jnp.mean(x * x, axis=-1, keepdims=True)
    h = (x * lax.rsqrt(ms + EPS) * g_ref[...]).astype(BF16)

    pa_ref[...] = _dot(h, wa_ref[...]).astype(BF16)
    pb_ref[...] = _dot(h, wb_ref[...]).astype(BF16)

    def split_heads(p, q_ref, k_ref, v_ref, gate_ref):
        for hh in range(N_HEADS):
            lo = hh * HEAD_PAD
            q_ref[hh] = p[:, lo:lo + HEAD_PAD].astype(BF16)
            k_ref[hh] = p[:, D_PAD + lo:D_PAD + lo + HEAD_PAD].astype(BF16)
            v_ref[hh] = p[:, 2 * D_PAD + lo:2 * D_PAD + lo + HEAD_PAD].astype(BF16)
        gate_ref[...] = p[:, 3 * D_PAD:4 * D_PAD].astype(BF16)

    split_heads(_dot(h, wc_ref[...]), qc_ref, kc_ref, vc_ref, gc_ref)
    split_heads(_dot(h, wd_ref[...]), qd_ref, kd_ref, vd_ref, gd_ref)

    f = _dot(h, wf_ref[...]) + bf_ref[...]
    ls = jnp.minimum(f, 0.0) - jnp.log(1.0 + jnp.exp(-jnp.abs(f)))
    row = lax.broadcasted_iota(jnp.int32, (tm, tm), 0)
    col = lax.broadcasted_iota(jnp.int32, (tm, tm), 1)
    tri = (row >= col).astype(F32)
    local = jnp.dot(tri, ls, precision=lax.Precision.HIGHEST, preferred_element_type=F32)

    @pl.when(i == 0)
    def _():
        carry_ref[...] = jnp.zeros_like(carry_ref)

    c = local + carry_ref[...]
    c_ref[...] = c
    carry_ref[...] = c[tm - 1:tm, :]


def _inproj(x, norm_g, wa, wb, wc, wf, wd, bf):
    b, s, d = x.shape
    tm = TM_IN
    const = lambda shape: pl.BlockSpec(shape, lambda bi, i: (0,) * len(shape))
    tok = lambda n: pl.BlockSpec((None, tm, n), lambda bi, i: (bi, i, 0))
    head = pl.BlockSpec((None, N_HEADS, tm, HEAD_PAD), lambda bi, i: (bi, 0, i, 0))
    head_shape = jax.ShapeDtypeStruct((b, N_HEADS, s, HEAD_PAD), BF16)
    return pl.pallas_call(
        _inproj_kernel,
        grid=(b, s // tm),
        in_specs=[tok(d), const((1, d)), const(wa.shape), const(wb.shape), const(wc.shape),
                  const(wf.shape), const(wd.shape), const((1, HEAD_PAD))],
        out_specs=[tok(wa.shape[1]), tok(wb.shape[1]), head, head, head, tok(D_PAD), tok(HEAD_PAD),
                   head, head, head, tok(D_PAD)],
        out_shape=[jax.ShapeDtypeStruct((b, s, wa.shape[1]), BF16),
                   jax.ShapeDtypeStruct((b, s, wb.shape[1]), BF16),
                   head_shape, head_shape, head_shape,
                   jax.ShapeDtypeStruct((b, s, D_PAD), BF16),
                   jax.ShapeDtypeStruct((b, s, HEAD_PAD), F32),
                   head_shape, head_shape, head_shape,
                   jax.ShapeDtypeStruct((b, s, D_PAD), BF16)],
        scratch_shapes=[pltpu.VMEM((1, HEAD_PAD), F32)],
        compiler_params=pltpu.CompilerParams(
            dimension_semantics=("arbitrary", "arbitrary"), vmem_limit_bytes=VMEM_LIMIT),
        name="inproj",
    )(x, norm_g, wa, wb, wc, wf, wd, bf)


def _chunk_attn_kernel(q_ref, k0_ref, k1_ref, k2_ref, v0_ref, v1_ref, v2_ref, bias_ref, o_ref):
    i = pl.program_id(1)
    tq = q_ref.shape[0]
    q = q_ref[...]
    k = jnp.concatenate([k0_ref[...], k1_ref[...], k2_ref[...]], axis=0)
    v = jnp.concatenate([v0_ref[...], v1_ref[...], v2_ref[...]], axis=0)
    nk = k.shape[0]
    head_of_lane = lax.broadcasted_iota(jnp.int32, (1, D_BRANCH), 1) // HEAD_DIM
    col = lax.broadcasted_iota(jnp.int32, (tq, nk), 1)
    in_seq = col >= LOOK_A - i * tq
    out = jnp.zeros((tq, D_BRANCH), F32)
    for hh in range(N_HEADS):
        sel = head_of_lane == hh
        qh = jnp.where(sel, q, jnp.zeros_like(q))
        s = _nt_dot(qh, k) + bias_ref[hh]
        s = jnp.where(in_seq, s, NEG)
        m = jnp.max(s, axis=-1, keepdims=True)
        p = jnp.exp(s - m)
        l = jnp.sum(p, axis=-1, keepdims=True)
        pv = _dot(p.astype(BF16), v)
        out = out + jnp.where(sel, pv * (1.0 / l), 0.0)
    o_ref[...] = out


def _chunk_attn(pa, bias):
    b, s, _ = pa.shape
    tq = TQ_A
    blk = lambda colblk, back: pl.BlockSpec(
        (None, tq, D_BRANCH), lambda bi, i: (bi, jnp.maximum(i - back, 0), colblk))
    return pl.pallas_call(
        _chunk_attn_kernel,
        grid=(b, s // tq),
        in_specs=[blk(0, 0), blk(1, 2), blk(1, 1), blk(1, 0), blk(2, 2), blk(2, 1), blk(2, 0),
                  pl.BlockSpec(bias.shape, lambda bi, i: (0, 0, 0))],
        out_specs=pl.BlockSpec((None, tq, D_BRANCH), lambda bi, i: (bi, i, 0)),
        out_shape=jax.ShapeDtypeStruct((b, s, D_BRANCH), F32),
        compiler_params=pltpu.CompilerParams(
            dimension_semantics=("arbitrary", "arbitrary"), vmem_limit_bytes=VMEM_LIMIT),
        name="chunk_attn",
    )(pa, pa, pa, pa, pa, pa, pa, bias)


def _chunk_bias(rel_bias):
    ti = np.arange(TQ_A)[:, None]
    sj = np.arange(LOOK_A + TQ_A)[None, :]
    rel = np.clip(ti - sj + LOOK_A, -MAX_REL, MAX_REL) + MAX_REL
    qc = ti // CHUNK
    kc = sj // CHUNK
    band = (kc >= qc) & (kc <= qc + LOOKBACK_CHUNKS)
    return jnp.where(band[None], rel_bias[:, rel].astype(F32), NEG)


def _fox_kernel(q_ref, k_ref, v_ref, cq_ref, ck_ref, o_ref, m_ref, l_ref, acc_ref):
    i = pl.program_id(2)
    tq = q_ref.shape[0]
    q = q_ref[...]
    cq = cq_ref[...]
    m_ref[...] = jnp.full_like(m_ref, NEG)
    l_ref[...] = jnp.zeros_like(l_ref)
    acc_ref[...] = jnp.zeros_like(acc_ref)

    def step(j, masked):
        ks = pl.multiple_of(j * tq, tq)
        k = k_ref[pl.ds(ks, tq), :]
        v = v_ref[pl.ds(ks, tq), :]
        ck = ck_ref[:, pl.ds(ks, tq)]
        s = _nt_dot(q, k) + (cq - ck)
        if masked:
            row = lax.broadcasted_iota(jnp.int32, (tq, tq), 0)
            col = lax.broadcasted_iota(jnp.int32, (tq, tq), 1)
            s = jnp.where(row >= col, s, NEG)
        m_prev = m_ref[...]
        m_new = jnp.maximum(m_prev, jnp.max(s, axis=-1, keepdims=True))
        alpha = jnp.exp(m_prev - m_new)
        p = jnp.exp(s - m_new)
        l_ref[...] = alpha * l_ref[...] + jnp.sum(p, axis=-1, keepdims=True)
        acc_ref[...] = alpha * acc_ref[...] + _dot(p.astype(BF16), v)
        m_ref[...] = m_new

    def body(j, carry):
        step(j, False)
        return carry

    lax.fori_loop(0, i, body, 0)
    step(i, True)
    o_ref[...] = acc_ref[...] * (1.0 / l_ref[...])


def _fox(q, k, v, c_col, c_row):
    b, nh, s, _ = q.shape
    tq = TQ_C
    qblk = pl.BlockSpec((None, None, tq, HEAD_PAD), lambda bi, hi, i: (bi, hi, i, 0))
    full = pl.BlockSpec((None, None, s, HEAD_PAD), lambda bi, hi, i: (bi, hi, 0, 0))
    return pl.pallas_call(
        _fox_kernel,
        grid=(b, nh, s // tq),
        in_specs=[qblk, full, full,
                  pl.BlockSpec((None, None, tq, 1), lambda bi, hi, i: (bi, hi, i, 0)),
                  pl.BlockSpec((None, None, 1, s), lambda bi, hi, i: (bi, hi, 0, 0))],
        out_specs=pl.BlockSpec((None, tq, HEAD_PAD), lambda bi, hi, i: (bi, i, hi)),
        out_shape=jax.ShapeDtypeStruct((b, s, D_PAD), F32),
        scratch_shapes=[pltpu.VMEM((tq, 1), F32), pltpu.VMEM((tq, 1), F32),
                        pltpu.VMEM((tq, HEAD_PAD), F32)],
        compiler_params=pltpu.CompilerParams(
            dimension_semantics=("arbitrary", "arbitrary", "arbitrary"),
            vmem_limit_bytes=VMEM_LIMIT),
        name="fox",
    )(q, k, v, c_col, c_row)


def _sb_kernel(q_ref, k_ref, v_ref, o_ref, r_ref, acc_ref):
    i = pl.program_id(2)
    tq = q_ref.shape[0]
    tk = TK_D
    per = tq // tk
    q = q_ref[...]
    r_ref[...] = jnp.zeros_like(r_ref)
    acc_ref[...] = jnp.zeros_like(acc_ref)
    trow = lax.broadcasted_iota(jnp.int32, (tk, tk), 0)
    tcol = lax.broadcasted_iota(jnp.int32, (tk, tk), 1)
    tri = (trow >= tcol).astype(BF16)

    def sub(kb, masked):
        ks = pl.multiple_of(kb * tk, tk)
        k = k_ref[pl.ds(ks, tk), :]
        v = v_ref[pl.ds(ks, tk), :]
        z = _nt_dot(q, k)
        sp = jnp.maximum(z, 0.0) + jnp.log(1.0 + jnp.exp(-jnp.abs(z)))
        if masked:
            qpos = i * tq + lax.broadcasted_iota(jnp.int32, (tq, tk), 0)
            kpos = ks + lax.broadcasted_iota(jnp.int32, (tq, tk), 1)
            msk = kpos < qpos
            sp = jnp.where(msk, sp, 0.0)
        hi = sp.astype(BF16)
        lo = (sp - hi.astype(F32)).astype(BF16)
        suf = _dot(hi, tri) + _dot(lo, tri)
        a = jnp.exp(z - suf - r_ref[...])
        if masked:
            a = jnp.where(msk, a, 0.0)
        acc_ref[...] += _dot(a.astype(BF16), v)
        r_ref[...] += suf[:, 0:1]

    for d in range(per):
        sub(i * per + (per - 1 - d), True)

    def body(jj, carry):
        base = (i - 1 - jj) * per
        for d in range(per):
            sub(base + (per - 1 - d), False)
        return carry

    lax.fori_loop(0, i, body, 0)
    o_ref[...] = acc_ref[...]


def _stick_breaking(q, k, v):
    b, nh, s, _ = q.shape
    tq = TQ_D
    qblk = pl.BlockSpec((None, None, tq, HEAD_PAD), lambda bi, hi, i: (bi, hi, i, 0))
    full = pl.BlockSpec((None, None, s, HEAD_PAD), lambda bi, hi, i: (bi, hi, 0, 0))
    return pl.pallas_call(
        _sb_kernel,
        grid=(b, nh, s // tq),
        in_specs=[qblk, full, full],
        out_specs=pl.BlockSpec((None, tq, HEAD_PAD), lambda bi, hi, i: (bi, i, hi)),
        out_shape=jax.ShapeDtypeStruct((b, s, D_PAD), F32),
        scratch_shapes=[pltpu.VMEM((tq, 1), F32), pltpu.VMEM((tq, HEAD_PAD), F32)],
        compiler_params=pltpu.CompilerParams(
            dimension_semantics=("arbitrary", "arbitrary", "arbitrary"),
            vmem_limit_bytes=VMEM_LIMIT),
        name="stick_breaking",
    )(q, k, v)


def _out_kernel(x_ref, ya_ref, ga_ref, pb_ref, yc_ref, gc_ref, yd_ref, gd_ref,
                bga_ref, bgb_ref, bgc_ref, bgd_ref, vg_ref, ws_ref, bs_ref,
                woa_ref, wob_ref, woc_ref, wod_ref, fg_ref, o_ref, *, last):
    tm = x_ref.shape[0]

    def gated(y, g_ref, gain_ref):
        ms = jnp.sum(y * y, axis=-1, keepdims=True) * (1.0 / D_BRANCH)
        g = g_ref[...].astype(F32)
        silu = g * (1.0 / (1.0 + jnp.exp(-g)))
        return (y * lax.rsqrt(ms + EPS) * gain_ref[...] * silu).astype(BF16)

    u = pb_ref[:, 0:D_BRANCH].astype(F32)
    vb = pb_ref[:, D_BRANCH:2 * D_BRANCH].astype(F32)
    mu = jnp.mean(vb, axis=-1, keepdims=True)
    xc = vb - mu
    var = jnp.mean(xc * xc, axis=-1, keepdims=True)
    vn = (xc * lax.rsqrt(var + EPS) * vg_ref[...]).astype(BF16)
    row = lax.broadcasted_iota(jnp.int32, (SG_CHUNK, SG_CHUNK), 0)
    col = lax.broadcasted_iota(jnp.int32, (SG_CHUNK, SG_CHUNK), 1)
    group_of_lane = lax.broadcasted_iota(jnp.int32, (1, D_BRANCH), 1) // HEAD_DIM
    ws = [jnp.where(row >= col, ws_ref[g], 0.0).astype(BF16) for g in range(N_HEADS)]
    mixed = []
    for ci in range(tm // SG_CHUNK):
        vchunk = vn[ci * SG_CHUNK:(ci + 1) * SG_CHUNK]
        acc = bs_ref[...]
        for g in range(N_HEADS):
            acc = acc + jnp.where(group_of_lane == g, _dot(ws[g], vchunk), 0.0)
        mixed.append(acc)
    yb = u * jnp.concatenate(mixed, axis=0)

    out = x_ref[...]
    out = out + _dot(gated(ya_ref[...], ga_ref, bga_ref), woa_ref[...])
    out = out + _dot(gated(yb, pb_ref.at[:, 2 * D_BRANCH:3 * D_BRANCH], bgb_ref), wob_ref[...])
    out = out + _dot(gated(yc_ref[...], gc_ref, bgc_ref), woc_ref[...])
    out = out + _dot(gated(yd_ref[...], gd_ref, bgd_ref), wod_ref[...])
    if last:
        ms = jnp.mean(out * out, axis=-1, keepdims=True)
        out = out * lax.rsqrt(ms + EPS) * fg_ref[...]
    o_ref[...] = out


def _out_proj(x, ya, pa, pb, yc, gc, yd, gd, bga, bgb, bgc, bgd, vg, ws, bs,
              woa, wob, woc, wod, fg, last):
    b, s, d = x.shape
    tm = TM_OUT
    tok = lambda n, colblk=0: pl.BlockSpec((None, tm, n), lambda bi, i: (bi, i, colblk))
    const = lambda a: pl.BlockSpec(a.shape, lambda bi, i: (0,) * a.ndim)
    return pl.pallas_call(
        functools.partial(_out_kernel, last=last),
        grid=(b, s // tm),
        in_specs=[tok(d), tok(D_BRANCH), tok(D_BRANCH, 3), tok(3 * D_BRANCH), tok(D_PAD), tok(D_PAD),
                  tok(D_PAD), tok(D_PAD), const(bga), const(bgb), const(bgc), const(bgd), const(vg),
                  const(ws), const(bs), const(woa), const(wob), const(woc), const(wod), const(fg)],
        out_specs=tok(d),
        out_shape=jax.ShapeDtypeStruct((b, s, d), F32),
        compiler_params=pltpu.CompilerParams(
            dimension_semantics=("arbitrary", "arbitrary"), vmem_limit_bytes=VMEM_LIMIT),
        name="out_proj",
    )(x, ya, pa, pb, yc, gc, yd, gd, bga, bgb, bgc, bgd, vg, ws, bs, woa, wob, woc, wod, fg)


def _pad_head_cols(w):
    lead = w.shape[:-1]
    w = w.reshape(lead + (N_HEADS, HEAD_DIM))
    w = jnp.pad(w, [(0, 0)] * len(lead) + [(0, 0), (0, HEAD_PAD - HEAD_DIM)])
    return w.reshape(lead + (D_PAD,))


def _pad_head_rows(w):
    return _pad_head_cols(w.T).T


def _layer(x, norm_g, w_in, b_f, rel_bias, w_s, b_s, v_gain, branch_gain, w_out, final_g, last):
    db = D_BRANCH
    o = 0
    wa = w_in[:, o:o + 4 * db]; o += 4 * db
    wb = w_in[:, o:o + 3 * db]; o += 3 * db
    wc = w_in[:, o:o + 4 * db]; o += 4 * db
    wf = w_in[:, o:o + N_HEADS]; o += N_HEADS
    wd = w_in[:, o:o + 4 * db]

    def padded(w):
        return jnp.concatenate(
            [_pad_head_cols(w[:, :db] * SCALE)] + [_pad_head_cols(w[:, j * db:(j + 1) * db]) for j in (1, 2, 3)],
            axis=1).astype(BF16)

    wa_s = jnp.concatenate([wa[:, :db] * SCALE, wa[:, db:]], axis=1).astype(BF16)
    wf_p = jnp.pad(wf, ((0, 0), (0, HEAD_PAD - N_HEADS))).astype(BF16)
    bf_p = jnp.pad(b_f, (0, HEAD_PAD - N_HEADS)).reshape(1, HEAD_PAD)

    pa, pb, qc, kc, vc, gc, c, qd, kd, vd, gd = _inproj(
        x, norm_g.reshape(1, -1), wa_s, wb.astype(BF16), padded(wc), wf_p, padded(wd), bf_p)

    ya = _chunk_attn(pa, _chunk_bias(rel_bias))
    ch = jnp.transpose(c[:, :, :N_HEADS], (0, 2, 1))
    yc = _fox(qc, kc, vc, ch[..., None], ch[:, :, None, :])
    yd = _stick_breaking(qd, kd, vd)

    bs_tile = jnp.repeat(jnp.transpose(b_s), HEAD_DIM, axis=1)
    return _out_proj(
        x, ya, pa, pb, yc, gc, yd, gd,
        branch_gain[0].reshape(1, -1), branch_gain[1].reshape(1, -1),
        _pad_head_cols(branch_gain[2]).reshape(1, -1), _pad_head_cols(branch_gain[3]).reshape(1, -1),
        v_gain.reshape(1, -1), w_s, bs_tile,
        w_out[0:db].astype(BF16), w_out[db:2 * db].astype(BF16),
        _pad_head_rows(w_out[2 * db:3 * db]).astype(BF16), _pad_head_rows(w_out[3 * db:4 * db]).astype(BF16),
        final_g.reshape(1, -1), last)


def kernel(x, norm_g, w_in, b_f, rel_bias, w_s, b_s, v_gain, branch_gain, w_out, final_g):
    depth = norm_g.shape[0]
    for l in range(depth):
        x = _layer(x, norm_g[l], w_in[l], b_f[l], rel_bias[l], w_s[l], b_s[l], v_gain[l],
                   branch_gain[l], w_out[l], final_g, last=(l == depth - 1))
    return x
```

```python
import functools

import jax
import jax.numpy as jnp
import numpy as np
from jax import lax
from jax.experimental import pallas as pl
from jax.experimental.pallas import tpu as pltpu

D_MODEL = 1024
N_HEADS = 4
HEAD_DIM = 64
D_BRANCH = N_HEADS * HEAD_DIM
CHUNK = 64
LOOKBACK_CHUNKS = 8
MAX_REL = 128
SG_CHUNK = 128
EPS = 1e-6
SCALE = HEAD_DIM ** -0.5
HEAD_PAD = 128
D_PAD = N_HEADS * HEAD_PAD
NEG = -1e30
LOG2E = 1.4426950408889634
SB_EXIT = 104.0

F32 = jnp.float32
BF16 = jnp.bfloat16

TM_IN = 256
TQ_A = 256
LOOK_A = LOOKBACK_CHUNKS * CHUNK
WIN_A = LOOK_A + TQ_A
ROLL_A = 1024
TQ_C = 512
QSPLIT_C = 256
TQ_D = 256
TM_OUT = 256
VMEM_LIMIT = 56 * 1024 * 1024


def _nt_dot(a, b):
    return lax.dot_general(a, b, (((1,), (1,)), ((), ())), preferred_element_type=F32)


def _dot(a, b):
    return jnp.dot(a, b, preferred_element_type=F32)


def _bf16_part(x):
    return x.astype(BF16).astype(F32)


def _inproj_kernel(x_ref, g_ref, wa_ref, wb_ref, wc_ref, wf_ref, wd_ref, bf_ref,
                   pa_ref, pb_ref, qc_ref, kc_ref, vc_ref, gc_ref,
                   qd_ref, kd_ref, vd_ref, gd_ref, carry_ref):
    i = pl.program_id(1)
    tm = x_ref.shape[0]
    x = x_ref[...]
    ms = jnp.mean(x * x, axis=-1, keepdims=True)
    h = (x * lax.rsqrt(ms + EPS) * g_ref[...]).astype(BF16)

    pa_ref[...] = _dot(h, wa_ref[...]).astype(BF16)
    pb_ref[...] = _dot(h, wb_ref[...]).astype(BF16)

    f = _dot(h, wf_ref[...]) + bf_ref[...]
    ls = jnp.minimum(f, 0.0) - jnp.log(1.0 + jnp.exp(-jnp.abs(f)))
    row = lax.broadcasted_iota(jnp.int32, (tm, tm), 0)
    col = lax.broadcasted_iota(jnp.int32, (tm, tm), 1)
    tri = (row >= col).astype(F32)
    local = jnp.dot(tri, ls, precision=lax.Precision.HIGHEST, preferred_element_type=F32)

    @pl.when(i == 0)
    def _():
        carry_ref[...] = jnp.zeros_like(carry_ref)

    c = local + carry_ref[...]
    carry_ref[...] = c[tm - 1:tm, :]
    c2 = c * LOG2E

    lane = lax.broadcasted_iota(jnp.int32, (1, HEAD_PAD), 1)
    ones3 = ((lane >= HEAD_DIM) & (lane < HEAD_DIM + 3)).astype(F32)
    ones3b = ((lane >= HEAD_DIM + 3) & (lane < HEAD_DIM + 6)).astype(F32)
    one_v = (lane == HEAD_DIM).astype(F32)

    pc = _dot(h, wc_ref[...])
    for hh in range(N_HEADS):
        lo = hh * HEAD_PAD
        ccol = jnp.sum(jnp.where(lane == hh, c2, 0.0), axis=1, keepdims=True)
        rep = jnp.broadcast_to(ccol, (tm, HEAD_PAD))
        c_hi = _bf16_part(rep)
        c_mid = _bf16_part(rep - c_hi)
        c_lo = rep - c_hi - c_mid
        pieces = (jnp.where(lane == HEAD_DIM, c_hi, 0.0) + jnp.where(lane == HEAD_DIM + 1, c_mid, 0.0)
                  + jnp.where(lane == HEAD_DIM + 2, c_lo, 0.0))
        npieces = -(jnp.where(lane == HEAD_DIM + 3, c_hi, 0.0) + jnp.where(lane == HEAD_DIM + 4, c_mid, 0.0)
                    + jnp.where(lane == HEAD_DIM + 5, c_lo, 0.0))
        qc_ref[hh] = (pc[:, lo:lo + HEAD_PAD] * LOG2E + pieces + ones3b).astype(BF16)
        kc_ref[hh] = (pc[:, D_PAD + lo:D_PAD + lo + HEAD_PAD] + npieces + ones3).astype(BF16)
        vc_ref[hh] = (pc[:, 2 * D_PAD + lo:2 * D_PAD + lo + HEAD_PAD] + one_v).astype(BF16)
    gc_ref[...] = pc[:, 3 * D_PAD:4 * D_PAD].astype(BF16)

    pd = _dot(h, wd_ref[...])
    for hh in range(N_HEADS):
        lo = hh * HEAD_PAD
        qd_ref[hh] = pd[:, lo:lo + HEAD_PAD].astype(BF16)
        kd_ref[hh] = pd[:, D_PAD + lo:D_PAD + lo + HEAD_PAD].astype(BF16)
        vd_ref[hh] = pd[:, 2 * D_PAD + lo:2 * D_PAD + lo + HEAD_PAD].astype(BF16)
    gd_ref[...] = pd[:, 3 * D_PAD:4 * D_PAD].astype(BF16)


def _inproj(x, norm_g, wa, wb, wc, wf, wd, bf):
    b, s, d = x.shape
    tm = TM_IN
    const = lambda shape: pl.BlockSpec(shape, lambda bi, i: (0,) * len(shape))
    tok = lambda n: pl.BlockSpec((None, tm, n), lambda bi, i: (bi, i, 0))
    head = pl.BlockSpec((None, N_HEADS, tm, HEAD_PAD), lambda bi, i: (bi, 0, i, 0))
    head_shape = jax.ShapeDtypeStruct((b, N_HEADS, s, HEAD_PAD), BF16)
    return pl.pallas_call(
        _inproj_kernel,
        grid=(b, s // tm),
        in_specs=[tok(d), const((1, d)), const(wa.shape), const(wb.shape), const(wc.shape),
                  const(wf.shape), const(wd.shape), const((1, HEAD_PAD))],
        out_specs=[tok(wa.shape[1]), tok(wb.shape[1]), head, head, head, tok(D_PAD),
                   head, head, head, tok(D_PAD)],
        out_shape=[jax.ShapeDtypeStruct((b, s, wa.shape[1]), BF16),
                   jax.ShapeDtypeStruct((b, s, wb.shape[1]), BF16),
                   head_shape, head_shape, head_shape,
                   jax.ShapeDtypeStruct((b, s, D_PAD), BF16),
                   head_shape, head_shape, head_shape,
                   jax.ShapeDtypeStruct((b, s, D_PAD), BF16)],
        scratch_shapes=[pltpu.VMEM((1, HEAD_PAD), F32)],
        compiler_params=pltpu.CompilerParams(
            dimension_semantics=("arbitrary", "arbitrary"), vmem_limit_bytes=VMEM_LIMIT),
        name="inproj",
    )(x, norm_g, wa, wb, wc, wf, wd, bf)


def _chunk_attn_kernel(q_ref, k0_ref, k1_ref, k2_ref, v0_ref, v1_ref, v2_ref, rb_ref, o_ref, bias_ref):
    i = pl.program_id(1)
    tq = q_ref.shape[0]

    @pl.when((pl.program_id(0) == 0) & (i == 0))
    def _():
        ti = lax.broadcasted_iota(jnp.int32, (tq, WIN_A), 0) // CHUNK
        sj = lax.broadcasted_iota(jnp.int32, (tq, WIN_A), 1) // CHUNK
        band = (sj >= ti) & (sj <= ti + LOOKBACK_CHUNKS)
        for hh in range(N_HEADS):
            rows = jnp.broadcast_to(rb_ref[hh:hh + 1, :], (tq, ROLL_A))
            toep = pltpu.roll(rows, 0, 1, stride=1, stride_axis=0)
            bias_ref[hh] = jnp.where(band, toep[:, :WIN_A], NEG)

    q = q_ref[...]
    k = jnp.concatenate([k0_ref[...], k1_ref[...], k2_ref[...]], axis=0)
    v = jnp.concatenate([v0_ref[...], v1_ref[...], v2_ref[...]], axis=0)
    head_of_lane = lax.broadcasted_iota(jnp.int32, (1, D_BRANCH), 1) // HEAD_DIM
    col = lax.broadcasted_iota(jnp.int32, (tq, WIN_A), 1)
    in_seq = col >= LOOK_A - i * tq
    out = jnp.zeros((tq, D_BRANCH), F32)
    for hh in range(N_HEADS):
        sel = head_of_lane == hh
        qh = jnp.where(sel, q, jnp.zeros_like(q))
        s = _nt_dot(qh, k) + bias_ref[hh]
        s = jnp.where(in_seq, s, NEG)
        m = jnp.max(s, axis=-1, keepdims=True)
        p = jnp.exp(s - m)
        l = jnp.sum(p, axis=-1, keepdims=True)
        pv = _dot(p.astype(BF16), v)
        out = out + jnp.where(sel, pv * (1.0 / l), 0.0)
    o_ref[...] = out


def _chunk_attn(pa, rb_row):
    b, s, _ = pa.shape
    tq = TQ_A
    blk = lambda colblk, back: pl.BlockSpec(
        (None, tq, D_BRANCH), lambda bi, i: (bi, jnp.maximum(i - back, 0), colblk))
    return pl.pallas_call(
        _chunk_attn_kernel,
        grid=(b, s // tq),
        in_specs=[blk(0, 0), blk(1, 2), blk(1, 1), blk(1, 0), blk(2, 2), blk(2, 1), blk(2, 0),
                  pl.BlockSpec(rb_row.shape, lambda bi, i: (0, 0))],
        out_specs=pl.BlockSpec((None, tq, D_BRANCH), lambda bi, i: (bi, i, 0)),
        out_shape=jax.ShapeDtypeStruct((b, s, D_BRANCH), F32),
        scratch_shapes=[pltpu.VMEM((N_HEADS, tq, WIN_A), F32)],
        compiler_params=pltpu.CompilerParams(
            dimension_semantics=("arbitrary", "arbitrary"), vmem_limit_bytes=VMEM_LIMIT),
        name="chunk_attn",
    )(pa, pa, pa, pa, pa, pa, pa, rb_row)


def _chunk_bias_row(rel_bias):
    u = np.arange(ROLL_A)
    e = np.where(u < WIN_A, u, u - ROLL_A)
    rel = np.clip(LOOK_A - e, -MAX_REL, MAX_REL) + MAX_REL
    return rel_bias[:, rel].astype(F32)


def _fox_kernel(q_ref, k_ref, vt_ref, o_ref, sa_ref, sb_ref, ma_ref, mb_ref, m_ref, acc_ref):
    i = pl.program_id(2)
    tq = q_ref.shape[0]
    q = q_ref[...]
    m_ref[...] = jnp.full_like(m_ref, NEG)
    acc_ref[...] = jnp.zeros_like(acc_ref)

    def scores(j, s_ref, cm_ref, masked):
        ks = pl.multiple_of(j * tq, tq)
        st = _nt_dot(k_ref[pl.ds(ks, tq), :], q)
        if masked:
            kp = lax.broadcasted_iota(jnp.int32, (tq, tq), 0)
            qp = lax.broadcasted_iota(jnp.int32, (tq, tq), 1)
            st = jnp.where(kp <= qp, st, NEG)
        s_ref[...] = st
        cm_ref[...] = jnp.max(st, axis=0, keepdims=True)

    def consume(j, s_ref, cm_ref):
        ks = pl.multiple_of(j * tq, tq)
        vt = vt_ref[:, pl.ds(ks, tq)]
        m_prev = m_ref[...]
        m_new = jnp.maximum(m_prev, cm_ref[...])
        alpha = jnp.exp2(m_prev - m_new)
        pt = jnp.exp2(s_ref[...] - m_new).astype(BF16)
        acc_ref[...] = alpha * acc_ref[...] + _dot(vt, pt)
        m_ref[...] = m_new

    scores(i, sa_ref, ma_ref, True)

    def pair(p, carry):
        j = i - 2 * p
        scores(j - 1, sb_ref, mb_ref, False)
        consume(j, sa_ref, ma_ref)
        scores(j - 2, sa_ref, ma_ref, False)
        consume(j - 1, sb_ref, mb_ref)
        return carry

    lax.fori_loop(0, i // 2, pair, 0)

    @pl.when(i % 2 == 1)
    def _():
        scores(0, sb_ref, mb_ref, False)
        consume(1, sa_ref, ma_ref)
        consume(0, sb_ref, mb_ref)

    @pl.when(i % 2 == 0)
    def _():
        consume(0, sa_ref, ma_ref)

    acc = acc_ref[...]
    out_t = acc * (1.0 / acc[HEAD_DIM:HEAD_DIM + 1, :])
    d_idx = lax.broadcasted_iota(jnp.int32, out_t.shape, 0)
    o_ref[...] = jnp.where(d_idx < HEAD_DIM, out_t, 0.0).T


def _fox(q, k, vt):
    b, nh, s, _ = q.shape
    tq = TQ_C
    qblk = pl.BlockSpec((None, None, tq, HEAD_PAD), lambda bi, hi, i: (bi, hi, i, 0))
    full = pl.BlockSpec((None, None, s, HEAD_PAD), lambda bi, hi, i: (bi, hi, 0, 0))
    full_t = pl.BlockSpec((None, None, HEAD_PAD, s), lambda bi, hi, i: (bi, hi, 0, 0))
    return pl.pallas_call(
        _fox_kernel,
        grid=(b, nh, s // tq),
        in_specs=[qblk, full, full_t],
        out_specs=pl.BlockSpec((None, tq, HEAD_PAD), lambda bi, hi, i: (bi, i, hi)),
        out_shape=jax.ShapeDtypeStruct((b, s, D_PAD), F32),
        scratch_shapes=[pltpu.VMEM((tq, tq), F32), pltpu.VMEM((tq, tq), F32),
                        pltpu.VMEM((1, tq), F32), pltpu.VMEM((1, tq), F32),
                        pltpu.VMEM((1, tq), F32), pltpu.VMEM((HEAD_PAD, tq), F32)],
        compiler_params=pltpu.CompilerParams(
            dimension_semantics=("arbitrary", "arbitrary", "arbitrary"),
            vmem_limit_bytes=VMEM_LIMIT),
        name="fox",
    )(q, k, vt)


def _sb_kernel(q_ref, k_ref, v_ref, o_ref, r_ref, acc_ref):
    i = pl.program_id(2)
    t = q_ref.shape[0]
    q = q_ref[...]
    row = lax.broadcasted_iota(jnp.int32, (t, t), 0)
    col = lax.broadcasted_iota(jnp.int32, (t, t), 1)
    tri = (row >= col).astype(BF16)

    def tile(kb, keep):
        ks = pl.multiple_of(kb * t, t)
        k = k_ref[pl.ds(ks, t), :]
        v = v_ref[pl.ds(ks, t), :]
        z = _nt_dot(q, k)
        sp = jnp.maximum(z, 0.0) + jnp.log(1.0 + jnp.exp(-jnp.abs(z)))
        if keep is not None:
            sp = jnp.where(keep, sp, 0.0)
        hi = sp.astype(BF16)
        lo = (sp - hi.astype(F32)).astype(BF16)
        suf = _dot(hi, tri) + _dot(lo, tri)
        return z, suf, v

    keep_a = col < row
    keep_b = i > 0
    z_a, suf_a, v_a = tile(i, keep_a)
    z_b, suf_b, v_b = tile(jnp.maximum(i - 1, 0), keep_b)
    tot_a = suf_a[:, 0:1]
    a_a = jnp.where(keep_a, jnp.exp(z_a - suf_a), 0.0)
    a_b = jnp.where(keep_b, jnp.exp(z_b - suf_b - tot_a), 0.0)
    acc_ref[...] = _dot(a_a.astype(BF16), v_a) + _dot(a_b.astype(BF16), v_b)
    r0 = tot_a + suf_b[:, 0:1]
    r_ref[...] = r0

    def cond(state):
        kb, rmin = state
        return (kb >= 0) & (rmin < SB_EXIT)

    def body(state):
        kb, _ = state
        z, suf, v = tile(kb, None)
        r = r_ref[...]
        a = jnp.exp(z - suf - r)
        acc_ref[...] += _dot(a.astype(BF16), v)
        r = r + suf[:, 0:1]
        r_ref[...] = r
        return kb - 1, jnp.min(r)

    lax.while_loop(cond, body, (i - 2, jnp.min(r0)))
    o_ref[...] = acc_ref[...]


def _stick_breaking(q, k, v):
    b, nh, s, _ = q.shape
    tq = TQ_D
    qblk = pl.BlockSpec((None, None, tq, HEAD_PAD), lambda bi, hi, i: (bi, hi, i, 0))
    full = pl.BlockSpec((None, None, s, HEAD_PAD), lambda bi, hi, i: (bi, hi, 0, 0))
    return pl.pallas_call(
        _sb_kernel,
        grid=(b, nh, s // tq),
        in_specs=[qblk, full, full],
        out_specs=pl.BlockSpec((None, tq, HEAD_PAD), lambda bi, hi, i: (bi, i, hi)),
        out_shape=jax.ShapeDtypeStruct((b, s, D_PAD), F32),
        scratch_shapes=[pltpu.VMEM((tq, 1), F32), pltpu.VMEM((tq, HEAD_PAD), F32)],
        compiler_params=pltpu.CompilerParams(
            dimension_semantics=("arbitrary", "arbitrary", "arbitrary"),
            vmem_limit_bytes=VMEM_LIMIT),
        name="stick_breaking",
    )(q, k, v)


def _out_kernel(x_ref, ya_ref, ga_ref, pb_ref, yc_ref, gc_ref, yd_ref, gd_ref,
                bga_ref, bgb_ref, bgc_ref, bgd_ref, vg_ref, ws_ref, bs_ref,
                woa_ref, wob_ref, woc_ref, wod_ref, fg_ref, o_ref, *, last):
    tm = x_ref.shape[0]

    def gated(y, g_ref, gain_ref):
        ms = jnp.sum(y * y, axis=-1, keepdims=True) * (1.0 / D_BRANCH)
        g = g_ref[...].astype(F32)
        silu = g * (1.0 / (1.0 + jnp.exp(-g)))
        return (y * lax.rsqrt(ms + EPS) * gain_ref[...] * silu).astype(BF16)

    u = pb_ref[:, 0:D_BRANCH].astype(F32)
    vb = pb_ref[:, D_BRANCH:2 * D_BRANCH].astype(F32)
    mu = jnp.mean(vb, axis=-1, keepdims=True)
    xc = vb - mu
    var = jnp.mean(xc * xc, axis=-1, keepdims=True)
    vn = (xc * lax.rsqrt(var + EPS) * vg_ref[...]).astype(BF16)
    row = lax.broadcasted_iota(jnp.int32, (SG_CHUNK, SG_CHUNK), 0)
    col = lax.broadcasted_iota(jnp.int32, (SG_CHUNK, SG_CHUNK), 1)
    group_of_lane = lax.broadcasted_iota(jnp.int32, (1, D_BRANCH), 1) // HEAD_DIM
    ws = [jnp.where(row >= col, ws_ref[g], 0.0).astype(BF16) for g in range(N_HEADS)]
    mixed = []
    for ci in range(tm // SG_CHUNK):
        vchunk = vn[ci * SG_CHUNK:(ci + 1) * SG_CHUNK]
        acc = bs_ref[...]
        for g in range(N_HEADS):
            acc = acc + jnp.where(group_of_lane == g, _dot(ws[g], vchunk), 0.0)
        mixed.append(acc)
    yb = u * jnp.concatenate(mixed, axis=0)

    out = x_ref[...]
    out = out + _dot(gated(ya_ref[...], ga_ref, bga_ref), woa_ref[...])
    out = out + _dot(gated(yb, pb_ref.at[:, 2 * D_BRANCH:3 * D_BRANCH], bgb_ref), wob_ref[...])
    out = out + _dot(gated(yc_ref[...], gc_ref, bgc_ref), woc_ref[...])
    out = out + _dot(gated(yd_ref[...], gd_ref, bgd_ref), wod_ref[...])
    if last:
        ms = jnp.mean(out * out, axis=-1, keepdims=True)
        out = out * lax.rsqrt(ms + EPS) * fg_ref[...]
    o_ref[...] = out


def _out_proj(x, ya, pa, pb, yc, gc, yd, gd, bga, bgb, bgc, bgd, vg, ws, bs,
              woa, wob, woc, wod, fg, last):
    b, s, d = x.shape
    tm = TM_OUT
    tok = lambda n, colblk=0: pl.BlockSpec((None, tm, n), lambda bi, i: (bi, i, colblk))
    const = lambda a: pl.BlockSpec(a.shape, lambda bi, i: (0,) * a.ndim)
    return pl.pallas_call(
        functools.partial(_out_kernel, last=last),
        grid=(b, s // tm),
        in_specs=[tok(d), tok(D_BRANCH), tok(D_BRANCH, 3), tok(3 * D_BRANCH), tok(D_PAD), tok(D_PAD),
                  tok(D_PAD), tok(D_PAD), const(bga), const(bgb), const(bgc), const(bgd), const(vg),
                  const(ws), const(bs), const(woa), const(wob), const(woc), const(wod), const(fg)],
        out_specs=tok(d),
        out_shape=jax.ShapeDtypeStruct((b, s, d), F32),
        compiler_params=pltpu.CompilerParams(
            dimension_semantics=("arbitrary", "arbitrary"), vmem_limit_bytes=VMEM_LIMIT),
        name="out_proj",
    )(x, ya, pa, pb, yc, gc, yd, gd, bga, bgb, bgc, bgd, vg, ws, bs, woa, wob, woc, wod, fg)


def _pad_head_cols(w):
    lead = w.shape[:-1]
    w = w.reshape(lead + (N_HEADS, HEAD_DIM))
    w = jnp.pad(w, [(0, 0)] * len(lead) + [(0, 0), (0, HEAD_PAD - HEAD_DIM)])
    return w.reshape(lead + (D_PAD,))


def _pad_head_rows(w):
    return _pad_head_cols(w.T).T


def _layer(x, norm_g, w_in, b_f, rel_bias, w_s, b_s, v_gain, branch_gain, w_out, final_g, last):
    db = D_BRANCH
    o = 0
    wa = w_in[:, o:o + 4 * db]; o += 4 * db
    wb = w_in[:, o:o + 3 * db]; o += 3 * db
    wc = w_in[:, o:o + 4 * db]; o += 4 * db
    wf = w_in[:, o:o + N_HEADS]; o += N_HEADS
    wd = w_in[:, o:o + 4 * db]

    def padded(w):
        return jnp.concatenate(
            [_pad_head_cols(w[:, :db] * SCALE)] + [_pad_head_cols(w[:, j * db:(j + 1) * db]) for j in (1, 2, 3)],
            axis=1).astype(BF16)

    wa_s = jnp.concatenate([wa[:, :db] * SCALE, wa[:, db:]], axis=1).astype(BF16)
    wf_p = jnp.pad(wf, ((0, 0), (0, HEAD_PAD - N_HEADS))).astype(BF16)
    bf_p = jnp.pad(b_f, (0, HEAD_PAD - N_HEADS)).reshape(1, HEAD_PAD)

    pa, pb, qc, kc, vc, gc, qd, kd, vd, gd = _inproj(
        x, norm_g.reshape(1, -1), wa_s, wb.astype(BF16), padded(wc), wf_p, padded(wd), bf_p)

    ya = _chunk_attn(pa, _chunk_bias_row(rel_bias))
    yc = _fox(qc, kc, jnp.swapaxes(vc, 2, 3))
    yd = _stick_breaking(qd, kd, vd)

    bs_tile = jnp.repeat(jnp.transpose(b_s), HEAD_DIM, axis=1)
    return _out_proj(
        x, ya, pa, pb, yc, gc, yd, gd,
        branch_gain[0].reshape(1, -1), branch_gain[1].reshape(1, -1),
        _pad_head_cols(branch_gain[2]).reshape(1, -1), _pad_head_cols(branch_gain[3]).reshape(1, -1),
        v_gain.reshape(1, -1), w_s, bs_tile,
        w_out[0:db].astype(BF16), w_out[db:2 * db].astype(BF16),
        _pad_head_rows(w_out[2 * db:3 * db]).astype(BF16), _pad_head_rows(w_out[3 * db:4 * db]).astype(BF16),
        final_g.reshape(1, -1), last)


def kernel(x, norm_g, w_in, b_f, rel_bias, w_s, b_s, v_gain, branch_gain, w_out, final_g):
    depth = norm_g.shape[0]
    for l in range(depth):
        x = _layer(x, norm_g[l], w_in[l], b_f[l], rel_bias[l], w_s[l], b_s[l], v_gain[l],
                   branch_gain[l], w_out[l], final_g, last=(l == depth - 1))
    return x
```

```python
import functools

import jax
import jax.numpy as jnp
import numpy as np
from jax import lax
from jax.experimental import pallas as pl
from jax.experimental.pallas import tpu as pltpu

D_MODEL = 1024
N_HEADS = 4
HEAD_DIM = 64
D_BRANCH = N_HEADS * HEAD_DIM
CHUNK = 64
LOOKBACK_CHUNKS = 8
MAX_REL = 128
SG_CHUNK = 128
EPS = 1e-6
SCALE = HEAD_DIM ** -0.5
HEAD_PAD = 128
D_PAD = N_HEADS * HEAD_PAD
NEG = -1e30
LOG2E = 1.4426950408889634
SB_EXIT = 104.0
FOX_EXIT = 152.0
NORM_SLACK = 1.01
NORM_ROWS = 2048

F32 = jnp.float32
BF16 = jnp.bfloat16

TM_IN = 256
TQ_A = 256
LOOK_A = LOOKBACK_CHUNKS * CHUNK
WIN_A = LOOK_A + TQ_A
ROLL_A = 1024
TQ_C = 512
QSPLIT_C = 256
TQ_D = 256
TM_OUT = 256
VMEM_LIMIT = 56 * 1024 * 1024


def _nt_dot(a, b):
    return lax.dot_general(a, b, (((1,), (1,)), ((), ())), preferred_element_type=F32)


def _dot(a, b):
    return jnp.dot(a, b, preferred_element_type=F32)


def _bf16_part(x):
    return x.astype(BF16).astype(F32)


def _inproj_kernel(x_ref, g_ref, wa_ref, wb_ref, wc_ref, wf_ref, wd_ref, bf_ref,
                   pa_ref, pb_ref, qc_ref, kc_ref, vc_ref, gc_ref, c_ref,
                   qd_ref, kd_ref, vd_ref, gd_ref, carry_ref):
    i = pl.program_id(1)
    tm = x_ref.shape[0]
    x = x_ref[...]
    ms = jnp.mean(x * x, axis=-1, keepdims=True)
    h = (x * lax.rsqrt(ms + EPS) * g_ref[...]).astype(BF16)

    pa_ref[...] = _dot(h, wa_ref[...]).astype(BF16)
    pb_ref[...] = _dot(h, wb_ref[...]).astype(BF16)

    f = _dot(h, wf_ref[...]) + bf_ref[...]
    ls = jnp.minimum(f, 0.0) - jnp.log(1.0 + jnp.exp(-jnp.abs(f)))
    row = lax.broadcasted_iota(jnp.int32, (tm, tm), 0)
    col = lax.broadcasted_iota(jnp.int32, (tm, tm), 1)
    tri = (row >= col).astype(F32)
    local = jnp.dot(tri, ls, precision=lax.Precision.HIGHEST, preferred_element_type=F32)

    @pl.when(i == 0)
    def _():
        carry_ref[...] = jnp.zeros_like(carry_ref)

    c = local + carry_ref[...]
    carry_ref[...] = c[tm - 1:tm, :]
    c2 = c * LOG2E
    c_ref[...] = c2

    lane = lax.broadcasted_iota(jnp.int32, (1, HEAD_PAD), 1)
    ones3 = ((lane >= HEAD_DIM) & (lane < HEAD_DIM + 3)).astype(F32)
    ones3b = ((lane >= HEAD_DIM + 3) & (lane < HEAD_DIM + 6)).astype(F32)
    one_v = (lane == HEAD_DIM).astype(F32)

    def head_tile(p, part, hh):
        lo = part * D_BRANCH + (hh // 2) * HEAD_PAD
        tile = p[:, lo:lo + HEAD_PAD]
        if hh % 2:
            tile = pltpu.roll(tile, HEAD_DIM, 1)
        return jnp.where(lane < HEAD_DIM, tile, 0.0)

    pc = _dot(h, wc_ref[...])
    for hh in range(N_HEADS):
        ccol = jnp.sum(jnp.where(lane == hh, c2, 0.0), axis=1, keepdims=True)
        rep = jnp.broadcast_to(ccol, (tm, HEAD_PAD))
        c_hi = _bf16_part(rep)
        c_mid = _bf16_part(rep - c_hi)
        c_lo = rep - c_hi - c_mid
        pieces = (jnp.where(lane == HEAD_DIM, c_hi, 0.0) + jnp.where(lane == HEAD_DIM + 1, c_mid, 0.0)
                  + jnp.where(lane == HEAD_DIM + 2, c_lo, 0.0))
        npieces = -(jnp.where(lane == HEAD_DIM + 3, c_hi, 0.0) + jnp.where(lane == HEAD_DIM + 4, c_mid, 0.0)
                    + jnp.where(lane == HEAD_DIM + 5, c_lo, 0.0))
        qc_ref[hh] = (head_tile(pc, 0, hh) * LOG2E + pieces + ones3b).astype(BF16)
        kc_ref[hh] = (head_tile(pc, 1, hh) + npieces + ones3).astype(BF16)
        vc_ref[hh] = (head_tile(pc, 2, hh) + one_v).astype(BF16)
    gc_ref[...] = pc[:, 3 * D_BRANCH:4 * D_BRANCH].astype(BF16)

    pd = _dot(h, wd_ref[...])
    for hh in range(N_HEADS):
        qd_ref[hh] = head_tile(pd, 0, hh).astype(BF16)
        kd_ref[hh] = head_tile(pd, 1, hh).astype(BF16)
        vd_ref[hh] = head_tile(pd, 2, hh).astype(BF16)
    gd_ref[...] = pd[:, 3 * D_BRANCH:4 * D_BRANCH].astype(BF16)


def _inproj(x, norm_g, wa, wb, wc, wf, wd, bf):
    b, s, d = x.shape
    tm = TM_IN
    const = lambda shape: pl.BlockSpec(shape, lambda bi, i: (0,) * len(shape))
    tok = lambda n: pl.BlockSpec((None, tm, n), lambda bi, i: (bi, i, 0))
    head = pl.BlockSpec((None, N_HEADS, tm, HEAD_PAD), lambda bi, i: (bi, 0, i, 0))
    head_shape = jax.ShapeDtypeStruct((b, N_HEADS, s, HEAD_PAD), BF16)
    return pl.pallas_call(
        _inproj_kernel,
        grid=(b, s // tm),
        in_specs=[tok(d), const((1, d)), const(wa.shape), const(wb.shape), const(wc.shape),
                  const(wf.shape), const(wd.shape), const((1, HEAD_PAD))],
        out_specs=[tok(wa.shape[1]), tok(wb.shape[1]), head, head, head, tok(D_BRANCH), tok(HEAD_PAD),
                   head, head, head, tok(D_BRANCH)],
        out_shape=[jax.ShapeDtypeStruct((b, s, wa.shape[1]), BF16),
                   jax.ShapeDtypeStruct((b, s, wb.shape[1]), BF16),
                   head_shape, head_shape, head_shape,
                   jax.ShapeDtypeStruct((b, s, D_BRANCH), BF16),
                   jax.ShapeDtypeStruct((b, s, HEAD_PAD), F32),
                   head_shape, head_shape, head_shape,
                   jax.ShapeDtypeStruct((b, s, D_BRANCH), BF16)],
        scratch_shapes=[pltpu.VMEM((1, HEAD_PAD), F32)],
        compiler_params=pltpu.CompilerParams(
            dimension_semantics=("arbitrary", "arbitrary"), vmem_limit_bytes=VMEM_LIMIT),
        name="inproj",
    )(x, norm_g, wa, wb, wc, wf, wd, bf)


def _chunk_attn_kernel(q_ref, k0_ref, k1_ref, k2_ref, v0_ref, v1_ref, v2_ref, rb_ref, o_ref, bias_ref):
    i = pl.program_id(1)
    tq = q_ref.shape[0]

    @pl.when((pl.program_id(0) == 0) & (i == 0))
    def _():
        ti = lax.broadcasted_iota(jnp.int32, (tq, WIN_A), 0) // CHUNK
        sj = lax.broadcasted_iota(jnp.int32, (tq, WIN_A), 1) // CHUNK
        band = (sj >= ti) & (sj <= ti + LOOKBACK_CHUNKS)
        for hh in range(N_HEADS):
            rows = jnp.broadcast_to(rb_ref[hh:hh + 1, :], (tq, ROLL_A))
            toep = pltpu.roll(rows, 0, 1, stride=1, stride_axis=0)
            bias_ref[hh] = jnp.where(band, toep[:, :WIN_A], NEG)

    q = q_ref[...]
    k = jnp.concatenate([k0_ref[...], k1_ref[...], k2_ref[...]], axis=0)
    v = jnp.concatenate([v0_ref[...], v1_ref[...], v2_ref[...]], axis=0)
    head_of_lane = lax.broadcasted_iota(jnp.int32, (1, D_BRANCH), 1) // HEAD_DIM
    col = lax.broadcasted_iota(jnp.int32, (tq, WIN_A), 1)
    in_seq = col >= LOOK_A - i * tq
    out = jnp.zeros((tq, D_BRANCH), F32)
    for hh in range(N_HEADS):
        sel = head_of_lane == hh
        qh = jnp.where(sel, q, jnp.zeros_like(q))
        s = _nt_dot(qh, k) + bias_ref[hh]
        s = jnp.where(in_seq, s, NEG)
        m = jnp.max(s, axis=-1, keepdims=True)
        p = jnp.exp(s - m)
        l = jnp.sum(p, axis=-1, keepdims=True)
        pv = _dot(p.astype(BF16), v)
        out = out + jnp.where(sel, pv * (1.0 / l), 0.0)
    o_ref[...] = out


def _chunk_attn(pa, rb_row):
    b, s, _ = pa.shape
    tq = TQ_A
    blk = lambda colblk, back: pl.BlockSpec(
        (None, tq, D_BRANCH), lambda bi, i: (bi, jnp.maximum(i - back, 0), colblk))
    return pl.pallas_call(
        _chunk_attn_kernel,
        grid=(b, s // tq),
        in_specs=[blk(0, 0), blk(1, 2), blk(1, 1), blk(1, 0), blk(2, 2), blk(2, 1), blk(2, 0),
                  pl.BlockSpec(rb_row.shape, lambda bi, i: (0, 0))],
        out_specs=pl.BlockSpec((None, tq, D_BRANCH), lambda bi, i: (bi, i, 0)),
        out_shape=jax.ShapeDtypeStruct((b, s, D_BRANCH), F32),
        scratch_shapes=[pltpu.VMEM((N_HEADS, tq, WIN_A), F32)],
        compiler_params=pltpu.CompilerParams(
            dimension_semantics=("arbitrary", "arbitrary"), vmem_limit_bytes=VMEM_LIMIT),
        name="chunk_attn",
    )(pa, pa, pa, pa, pa, pa, pa, rb_row)


def _chunk_bias_row(rel_bias):
    u = np.arange(ROLL_A)
    e = np.where(u < WIN_A, u, u - ROLL_A)
    rel = np.clip(LOOK_A - e, -MAX_REL, MAX_REL) + MAX_REL
    return rel_bias[:, rel].astype(F32)


def _fox_kernel(cend_ref, q_ref, k_ref, vt_ref, cq_ref, o_ref,
                sa_ref, sb_ref, ma_ref, mb_ref, m_ref, acc_ref, kmax_ref):
    bi, hi, i = pl.program_id(0), pl.program_id(1), pl.program_id(2)
    tq = q_ref.shape[0]
    s_len = k_ref.shape[0]
    q = q_ref[...]
    m_ref[...] = jnp.full_like(m_ref, NEG)
    acc_ref[...] = jnp.zeros_like(acc_ref)
    lane = lax.broadcasted_iota(jnp.int32, (1, HEAD_PAD), 1)

    @pl.when(i == 0)
    def _():
        def blk(r, mx):
            kf = k_ref[pl.ds(pl.multiple_of(r * NORM_ROWS, NORM_ROWS), NORM_ROWS), :].astype(F32)
            k2 = jnp.sum(jnp.where(lane < HEAD_DIM, kf * kf, 0.0), axis=1, keepdims=True)
            return jnp.maximum(mx, jnp.max(k2))
        kmax_ref[0] = jnp.sqrt(lax.fori_loop(0, s_len // NORM_ROWS, blk, jnp.float32(0.0)))

    qf = q.astype(F32)
    q2 = jnp.where(lane < HEAD_DIM, qf * qf, 0.0).astype(BF16)
    n2 = _nt_dot(jnp.ones((8, HEAD_PAD), BF16), q2)[0:1, :]
    reach = jnp.sqrt(n2) * (kmax_ref[0] * NORM_SLACK) + cq_ref[...]
    cbase = (bi * pl.num_programs(1) + hi) * pl.num_programs(2)

    def scores(j, s_ref, cm_ref, masked):
        ks = pl.multiple_of(j * tq, tq)
        st = _nt_dot(k_ref[pl.ds(ks, tq), :], q)
        if masked:
            kp = lax.broadcasted_iota(jnp.int32, (tq, tq), 0)
            qp = lax.broadcasted_iota(jnp.int32, (tq, tq), 1)
            st = jnp.where(kp <= qp, st, NEG)
        s_ref[...] = st
        cm_ref[...] = jnp.max(st, axis=0, keepdims=True)

    def consume(j, s_ref, cm_ref):
        ks = pl.multiple_of(j * tq, tq)
        vt = vt_ref[:, pl.ds(ks, tq)]
        m_prev = m_ref[...]
        m_new = jnp.maximum(m_prev, cm_ref[...])
        alpha = jnp.exp2(m_prev - m_new)
        pt = jnp.exp2(s_ref[...] - m_new).astype(BF16)
        acc_ref[...] = alpha * acc_ref[...] + _dot(vt, pt)
        m_ref[...] = m_new

    scores(i, sa_ref, ma_ref, True)
    n_pairs = i // 2

    def live(p, g):
        return g - cend_ref[cbase + jnp.maximum(i - 2 * p, 0)] >= -FOX_EXIT

    def cond(state):
        p, g = state
        return (p < n_pairs) & live(p, g)

    def pair(state):
        p, _ = state
        g = jnp.max(reach - m_ref[...])
        j = i - 2 * p
        scores(j - 1, sb_ref, mb_ref, False)
        consume(j, sa_ref, ma_ref)
        scores(j - 2, sa_ref, ma_ref, False)
        consume(j - 1, sb_ref, mb_ref)
        return p + 1, g

    p_end, g_end = lax.while_loop(cond, pair, (jnp.int32(0), jnp.float32(-NEG)))
    finished = (p_end == n_pairs) & live(p_end, g_end)

    @pl.when(finished & (i % 2 == 1))
    def _():
        scores(0, sb_ref, mb_ref, False)
        consume(1, sa_ref, ma_ref)
        consume(0, sb_ref, mb_ref)

    @pl.when(finished & (i % 2 == 0))
    def _():
        consume(0, sa_ref, ma_ref)

    acc = acc_ref[...]
    out_t = acc * (1.0 / acc[HEAD_DIM:HEAD_DIM + 1, :])
    d_idx = lax.broadcasted_iota(jnp.int32, out_t.shape, 0)
    o_ref[...] = jnp.where(d_idx < HEAD_DIM, out_t, 0.0).T


def _fox(q, k, vt, c_row):
    b, nh, s, _ = q.shape
    tq = TQ_C
    cend = c_row[:, :, 0, tq - 1::tq].reshape(-1)
    qblk = pl.BlockSpec((None, None, tq, HEAD_PAD), lambda bi, hi, i, ce: (bi, hi, i, 0))
    full = pl.BlockSpec((None, None, s, HEAD_PAD), lambda bi, hi, i, ce: (bi, hi, 0, 0))
    full_t = pl.BlockSpec((None, None, HEAD_PAD, s), lambda bi, hi, i, ce: (bi, hi, 0, 0))
    crow = pl.BlockSpec((None, None, 1, tq), lambda bi, hi, i, ce: (bi, hi, 0, i))
    return pl.pallas_call(
        _fox_kernel,
        grid_spec=pltpu.PrefetchScalarGridSpec(
            num_scalar_prefetch=1,
            grid=(b, nh, s // tq),
            in_specs=[qblk, full, full_t, crow],
            out_specs=pl.BlockSpec((None, tq, HEAD_PAD), lambda bi, hi, i, ce: (bi, i, hi)),
            scratch_shapes=[pltpu.VMEM((tq, tq), F32), pltpu.VMEM((tq, tq), F32),
                            pltpu.VMEM((1, tq), F32), pltpu.VMEM((1, tq), F32),
                            pltpu.VMEM((1, tq), F32), pltpu.VMEM((HEAD_PAD, tq), F32),
                            pltpu.SMEM((1,), F32)]),
        out_shape=jax.ShapeDtypeStruct((b, s, D_PAD), F32),
        compiler_params=pltpu.CompilerParams(
            dimension_semantics=("arbitrary", "arbitrary", "arbitrary"),
            vmem_limit_bytes=VMEM_LIMIT),
        name="fox",
    )(cend, q, k, vt, c_row)


def _sb_kernel(q_ref, k_ref, v_ref, o_ref, r_ref, acc_ref):
    i = pl.program_id(2)
    t = q_ref.shape[0]
    q = q_ref[...]
    row = lax.broadcasted_iota(jnp.int32, (t, t), 0)
    col = lax.broadcasted_iota(jnp.int32, (t, t), 1)
    tri = (row >= col).astype(BF16)

    def tile(kb, keep):
        ks = pl.multiple_of(kb * t, t)
        k = k_ref[pl.ds(ks, t), :]
        v = v_ref[pl.ds(ks, t), :]
        z = _nt_dot(q, k)
        sp = jnp.maximum(z, 0.0) + jnp.log(1.0 + jnp.exp(-jnp.abs(z)))
        if keep is not None:
            sp = jnp.where(keep, sp, 0.0)
        hi = sp.astype(BF16)
        lo = (sp - hi.astype(F32)).astype(BF16)
        suf = _dot(hi, tri) + _dot(lo, tri)
        return z, suf, v

    keep_a = col < row
    keep_b = i > 0
    z_a, suf_a, v_a = tile(i, keep_a)
    z_b, suf_b, v_b = tile(jnp.maximum(i - 1, 0), keep_b)
    tot_a = suf_a[:, 0:1]
    a_a = jnp.where(keep_a, jnp.exp(z_a - suf_a), 0.0)
    a_b = jnp.where(keep_b, jnp.exp(z_b - suf_b - tot_a), 0.0)
    acc_ref[...] = _dot(a_a.astype(BF16), v_a) + _dot(a_b.astype(BF16), v_b)
    r0 = tot_a + suf_b[:, 0:1]
    r_ref[...] = r0

    def cond(state):
        kb, rmin = state
        return (kb >= 0) & (rmin < SB_EXIT)

    def body(state):
        kb, _ = state
        z, suf, v = tile(kb, None)
        r = r_ref[...]
        a = jnp.exp(z - suf - r)
        acc_ref[...] += _dot(a.astype(BF16), v)
        r = r + suf[:, 0:1]
        r_ref[...] = r
        return kb - 1, jnp.min(r)

    lax.while_loop(cond, body, (i - 2, jnp.min(r0)))
    o_ref[...] = acc_ref[...]


def _stick_breaking(q, k, v):
    b, nh, s, _ = q.shape
    tq = TQ_D
    qblk = pl.BlockSpec((None, None, tq, HEAD_PAD), lambda bi, hi, i: (bi, hi, i, 0))
    full = pl.BlockSpec((None, None, s, HEAD_PAD), lambda bi, hi, i: (bi, hi, 0, 0))
    return pl.pallas_call(
        _sb_kernel,
        grid=(b, nh, s // tq),
        in_specs=[qblk, full, full],
        out_specs=pl.BlockSpec((None, tq, HEAD_PAD), lambda bi, hi, i: (bi, i, hi)),
        out_shape=jax.ShapeDtypeStruct((b, s, D_PAD), F32),
        scratch_shapes=[pltpu.VMEM((tq, 1), F32), pltpu.VMEM((tq, HEAD_PAD), F32)],
        compiler_params=pltpu.CompilerParams(
            dimension_semantics=("arbitrary", "arbitrary", "arbitrary"),
            vmem_limit_bytes=VMEM_LIMIT),
        name="stick_breaking",
    )(q, k, v)


def _out_kernel(x_ref, ya_ref, ga_ref, pb_ref, yc_ref, gc_ref, yd_ref, gd_ref,
                bga_ref, bgb_ref, bgc_ref, bgd_ref, vg_ref, ws_ref, bs_ref,
                woa_ref, wob_ref, woc_ref, wod_ref, fg_ref, o_ref, *, last):
    tm = x_ref.shape[0]

    def gated(y, g_ref, gain_ref):
        ms = jnp.sum(y * y, axis=-1, keepdims=True) * (1.0 / D_BRANCH)
        g = g_ref[...].astype(F32)
        silu = g * (1.0 / (1.0 + jnp.exp(-g)))
        return (y * lax.rsqrt(ms + EPS) * gain_ref[...] * silu).astype(BF16)

    u = pb_ref[:, 0:D_BRANCH].astype(F32)
    vb = pb_ref[:, D_BRANCH:2 * D_BRANCH].astype(F32)
    mu = jnp.mean(vb, axis=-1, keepdims=True)
    xc = vb - mu
    var = jnp.mean(xc * xc, axis=-1, keepdims=True)
    vn = (xc * lax.rsqrt(var + EPS) * vg_ref[...]).astype(BF16)
    row = lax.broadcasted_iota(jnp.int32, (SG_CHUNK, SG_CHUNK), 0)
    col = lax.broadcasted_iota(jnp.int32, (SG_CHUNK, SG_CHUNK), 1)
    group_of_lane = lax.broadcasted_iota(jnp.int32, (1, D_BRANCH), 1) // HEAD_DIM
    ws = [jnp.where(row >= col, ws_ref[g], 0.0).astype(BF16) for g in range(N_HEADS)]
    mixed = []
    for ci in range(tm // SG_CHUNK):
        vchunk = vn[ci * SG_CHUNK:(ci + 1) * SG_CHUNK]
        acc = bs_ref[...]
        for g in range(N_HEADS):
            acc = acc + jnp.where(group_of_lane == g, _dot(ws[g], vchunk), 0.0)
        mixed.append(acc)
    yb = u * jnp.concatenate(mixed, axis=0)

    out = x_ref[...]
    out = out + _dot(gated(ya_ref[...], ga_ref, bga_ref), woa_ref[...])
    out = out + _dot(gated(yb, pb_ref.at[:, 2 * D_BRANCH:3 * D_BRANCH], bgb_ref), wob_ref[...])
    lane = lax.broadcasted_iota(jnp.int32, (1, HEAD_PAD), 1)

    def unpad(y_ref):
        halves = []
        for pr in range(N_HEADS // 2):
            even = y_ref[:, (2 * pr) * HEAD_PAD:(2 * pr + 1) * HEAD_PAD]
            odd = y_ref[:, (2 * pr + 1) * HEAD_PAD:(2 * pr + 2) * HEAD_PAD]
            halves.append(jnp.where(lane < HEAD_DIM, even, pltpu.roll(odd, HEAD_DIM, 1)))
        return jnp.concatenate(halves, axis=1)

    out = out + _dot(gated(unpad(yc_ref), gc_ref, bgc_ref), woc_ref[...])
    out = out + _dot(gated(unpad(yd_ref), gd_ref, bgd_ref), wod_ref[...])
    if last:
        ms = jnp.mean(out * out, axis=-1, keepdims=True)
        out = out * lax.rsqrt(ms + EPS) * fg_ref[...]
    o_ref[...] = out


def _out_proj(x, ya, pa, pb, yc, gc, yd, gd, bga, bgb, bgc, bgd, vg, ws, bs,
              woa, wob, woc, wod, fg, last):
    b, s, d = x.shape
    tm = TM_OUT
    tok = lambda n, colblk=0: pl.BlockSpec((None, tm, n), lambda bi, i: (bi, i, colblk))
    const = lambda a: pl.BlockSpec(a.shape, lambda bi, i: (0,) * a.ndim)
    return pl.pallas_call(
        functools.partial(_out_kernel, last=last),
        grid=(b, s // tm),
        in_specs=[tok(d), tok(D_BRANCH), tok(D_BRANCH, 3), tok(3 * D_BRANCH), tok(D_PAD), tok(D_BRANCH),
                  tok(D_PAD), tok(D_BRANCH), const(bga), const(bgb), const(bgc), const(bgd), const(vg),
                  const(ws), const(bs), const(woa), const(wob), const(woc), const(wod), const(fg)],
        out_specs=tok(d),
        out_shape=jax.ShapeDtypeStruct((b, s, d), F32),
        compiler_params=pltpu.CompilerParams(
            dimension_semantics=("arbitrary", "arbitrary"), vmem_limit_bytes=VMEM_LIMIT),
        name="out_proj",
    )(x, ya, pa, pb, yc, gc, yd, gd, bga, bgb, bgc, bgd, vg, ws, bs, woa, wob, woc, wod, fg)


def _layer(x, norm_g, w_in, b_f, rel_bias, w_s, b_s, v_gain, branch_gain, w_out, final_g, last):
    db = D_BRANCH
    o = 0
    wa = w_in[:, o:o + 4 * db]; o += 4 * db
    wb = w_in[:, o:o + 3 * db]; o += 3 * db
    wc = w_in[:, o:o + 4 * db]; o += 4 * db
    wf = w_in[:, o:o + N_HEADS]; o += N_HEADS
    wd = w_in[:, o:o + 4 * db]

    def q_scaled(w):
        return jnp.concatenate([w[:, :db] * SCALE, w[:, db:]], axis=1).astype(BF16)

    wf_p = jnp.pad(wf, ((0, 0), (0, HEAD_PAD - N_HEADS))).astype(BF16)
    bf_p = jnp.pad(b_f, (0, HEAD_PAD - N_HEADS)).reshape(1, HEAD_PAD)

    pa, pb, qc, kc, vc, gc, c2, qd, kd, vd, gd = _inproj(
        x, norm_g.reshape(1, -1), q_scaled(wa), wb.astype(BF16), q_scaled(wc), wf_p, q_scaled(wd), bf_p)

    ya = _chunk_attn(pa, _chunk_bias_row(rel_bias))
    c_row = jnp.transpose(c2[:, :, :N_HEADS], (0, 2, 1))[:, :, None, :]
    yc = _fox(qc, kc, jnp.swapaxes(vc, 2, 3), c_row)
    yd = _stick_breaking(qd, kd, vd)

    bs_tile = jnp.repeat(jnp.transpose(b_s), HEAD_DIM, axis=1)
    return _out_proj(
        x, ya, pa, pb, yc, gc, yd, gd,
        branch_gain[0].reshape(1, -1), branch_gain[1].reshape(1, -1),
        branch_gain[2].reshape(1, -1), branch_gain[3].reshape(1, -1),
        v_gain.reshape(1, -1), w_s, bs_tile,
        w_out[0:db].astype(BF16), w_out[db:2 * db].astype(BF16),
        w_out[2 * db:3 * db].astype(BF16), w_out[3 * db:4 * db].astype(BF16),
        final_g.reshape(1, -1), last)


def kernel(x, norm_g, w_in, b_f, rel_bias, w_s, b_s, v_gain, branch_gain, w_out, final_g):
    depth = norm_g.shape[0]
    for l in range(depth):
        x = _layer(x, norm_g[l], w_in[l], b_f[l], rel_bias[l], w_s[l], b_s[l], v_gain[l],
                   branch_gain[l], w_out[l], final_g, last=(l == depth - 1))
    return x
```

```python
import functools

import jax
import jax.numpy as jnp
import numpy as np
from jax import lax
from jax.experimental import pallas as pl
from jax.experimental.pallas import tpu as pltpu

D_MODEL = 1024
N_HEADS = 4
HEAD_DIM = 64
D_BRANCH = N_HEADS * HEAD_DIM
CHUNK = 64
LOOKBACK_CHUNKS = 8
MAX_REL = 128
SG_CHUNK = 128
EPS = 1e-6
SCALE = HEAD_DIM ** -0.5
HEAD_PAD = 128
D_PAD = N_HEADS * HEAD_PAD
NEG = -1e30
LOG2E = 1.4426950408889634
SB_EXIT = 151.0
FOX_EXIT = 152.0
NORM_SLACK = 1.01
NORM_ROWS = 2048

F32 = jnp.float32
BF16 = jnp.bfloat16

TM_IN = 256
TQ_A = 256
LOOK_A = LOOKBACK_CHUNKS * CHUNK
WIN_A = LOOK_A + TQ_A
ROLL_A = 1024
TQ_C = 512
QSPLIT_C = 256
TQ_D = 256
TM_OUT = 256
VMEM_LIMIT = 56 * 1024 * 1024


def _nt_dot(a, b):
    return lax.dot_general(a, b, (((1,), (1,)), ((), ())), preferred_element_type=F32)


def _dot(a, b):
    return jnp.dot(a, b, preferred_element_type=F32)


def _bf16_part(x):
    return x.astype(BF16).astype(F32)


def _inproj_kernel(x_ref, g_ref, wa_ref, wb_ref, wc_ref, wf_ref, wd_ref, bf_ref,
                   pa_ref, pb_ref, qc_ref, kc_ref, vc_ref, gc_ref, c_ref,
                   qd_ref, kd_ref, vd_ref, gd_ref, carry_ref):
    i = pl.program_id(1)
    tm = x_ref.shape[0]
    x = x_ref[...]
    ms = jnp.mean(x * x, axis=-1, keepdims=True)
    h = (x * lax.rsqrt(ms + EPS) * g_ref[...]).astype(BF16)

    pa_ref[...] = _dot(h, wa_ref[...]).astype(BF16)
    pb_ref[...] = _dot(h, wb_ref[...]).astype(BF16)

    f = _dot(h, wf_ref[...]) + bf_ref[...]
    ls = jnp.minimum(f, 0.0) - jnp.log(1.0 + jnp.exp(-jnp.abs(f)))
    row = lax.broadcasted_iota(jnp.int32, (tm, tm), 0)
    col = lax.broadcasted_iota(jnp.int32, (tm, tm), 1)
    tri = (row >= col).astype(F32)
    local = jnp.dot(tri, ls, precision=lax.Precision.HIGHEST, preferred_element_type=F32)

    @pl.when(i == 0)
    def _():
        carry_ref[...] = jnp.zeros_like(carry_ref)

    c = local + carry_ref[...]
    carry_ref[...] = c[tm - 1:tm, :]
    c2 = c * LOG2E
    c_ref[...] = c2

    lane = lax.broadcasted_iota(jnp.int32, (1, HEAD_PAD), 1)
    ones3 = ((lane >= HEAD_DIM) & (lane < HEAD_DIM + 3)).astype(F32)
    ones3b = ((lane >= HEAD_DIM + 3) & (lane < HEAD_DIM + 6)).astype(F32)
    one_v = (lane == HEAD_DIM).astype(F32)

    def head_tile(p, part, hh):
        lo = part * D_BRANCH + (hh // 2) * HEAD_PAD
        tile = p[:, lo:lo + HEAD_PAD]
        if hh % 2:
            tile = pltpu.roll(tile, HEAD_DIM, 1)
        return jnp.where(lane < HEAD_DIM, tile, 0.0)

    pc = _dot(h, wc_ref[...])
    for hh in range(N_HEADS):
        ccol = jnp.sum(jnp.where(lane == hh, c2, 0.0), axis=1, keepdims=True)
        rep = jnp.broadcast_to(ccol, (tm, HEAD_PAD))
        c_hi = _bf16_part(rep)
        c_mid = _bf16_part(rep - c_hi)
        c_lo = rep - c_hi - c_mid
        pieces = (jnp.where(lane == HEAD_DIM, c_hi, 0.0) + jnp.where(lane == HEAD_DIM + 1, c_mid, 0.0)
                  + jnp.where(lane == HEAD_DIM + 2, c_lo, 0.0))
        npieces = -(jnp.where(lane == HEAD_DIM + 3, c_hi, 0.0) + jnp.where(lane == HEAD_DIM + 4, c_mid, 0.0)
                    + jnp.where(lane == HEAD_DIM + 5, c_lo, 0.0))
        qc_ref[hh] = (head_tile(pc, 0, hh) * LOG2E + pieces + ones3b).astype(BF16)
        kc_ref[hh] = (head_tile(pc, 1, hh) + npieces + ones3).astype(BF16)
        vc_ref[hh] = (head_tile(pc, 2, hh) + one_v).astype(BF16)
    gc_ref[...] = pc[:, 3 * D_BRANCH:4 * D_BRANCH].astype(BF16)

    pd = _dot(h, wd_ref[...])
    for hh in range(N_HEADS):
        qd_ref[hh] = (head_tile(pd, 0, hh) * LOG2E).astype(BF16)
        kd_ref[hh] = head_tile(pd, 1, hh).astype(BF16)
        vd_ref[hh] = head_tile(pd, 2, hh).astype(BF16)
    gd_ref[...] = pd[:, 3 * D_BRANCH:4 * D_BRANCH].astype(BF16)


def _inproj(x, norm_g, wa, wb, wc, wf, wd, bf):
    b, s, d = x.shape
    tm = TM_IN
    const = lambda shape: pl.BlockSpec(shape, lambda bi, i: (0,) * len(shape))
    tok = lambda n: pl.BlockSpec((None, tm, n), lambda bi, i: (bi, i, 0))
    head = pl.BlockSpec((None, N_HEADS, tm, HEAD_PAD), lambda bi, i: (bi, 0, i, 0))
    head_shape = jax.ShapeDtypeStruct((b, N_HEADS, s, HEAD_PAD), BF16)
    return pl.pallas_call(
        _inproj_kernel,
        grid=(b, s // tm),
        in_specs=[tok(d), const((1, d)), const(wa.shape), const(wb.shape), const(wc.shape),
                  const(wf.shape), const(wd.shape), const((1, HEAD_PAD))],
        out_specs=[tok(wa.shape[1]), tok(wb.shape[1]), head, head, head, tok(D_BRANCH), tok(HEAD_PAD),
                   head, head, head, tok(D_BRANCH)],
        out_shape=[jax.ShapeDtypeStruct((b, s, wa.shape[1]), BF16),
                   jax.ShapeDtypeStruct((b, s, wb.shape[1]), BF16),
                   head_shape, head_shape, head_shape,
                   jax.ShapeDtypeStruct((b, s, D_BRANCH), BF16),
                   jax.ShapeDtypeStruct((b, s, HEAD_PAD), F32),
                   head_shape, head_shape, head_shape,
                   jax.ShapeDtypeStruct((b, s, D_BRANCH), BF16)],
        scratch_shapes=[pltpu.VMEM((1, HEAD_PAD), F32)],
        compiler_params=pltpu.CompilerParams(
            dimension_semantics=("arbitrary", "arbitrary"), vmem_limit_bytes=VMEM_LIMIT),
        name="inproj",
    )(x, norm_g, wa, wb, wc, wf, wd, bf)


def _chunk_attn_kernel(q_ref, k0_ref, k1_ref, k2_ref, v0_ref, v1_ref, v2_ref, rb_ref, o_ref, bias_ref):
    i = pl.program_id(1)
    tq = q_ref.shape[0]

    @pl.when((pl.program_id(0) == 0) & (i == 0))
    def _():
        ti = lax.broadcasted_iota(jnp.int32, (tq, WIN_A), 0) // CHUNK
        sj = lax.broadcasted_iota(jnp.int32, (tq, WIN_A), 1) // CHUNK
        band = (sj >= ti) & (sj <= ti + LOOKBACK_CHUNKS)
        for hh in range(N_HEADS):
            rows = jnp.broadcast_to(rb_ref[hh:hh + 1, :], (tq, ROLL_A))
            toep = pltpu.roll(rows, 0, 1, stride=1, stride_axis=0)
            bias_ref[hh] = jnp.where(band, toep[:, :WIN_A], NEG)

    q = q_ref[...]
    k = jnp.concatenate([k0_ref[...], k1_ref[...], k2_ref[...]], axis=0)
    v = jnp.concatenate([v0_ref[...], v1_ref[...], v2_ref[...]], axis=0)
    head_of_lane = lax.broadcasted_iota(jnp.int32, (1, D_BRANCH), 1) // HEAD_DIM
    col = lax.broadcasted_iota(jnp.int32, (tq, WIN_A), 1)
    in_seq = col >= LOOK_A - i * tq
    out = jnp.zeros((tq, D_BRANCH), F32)
    for hh in range(N_HEADS):
        sel = head_of_lane == hh
        qh = jnp.where(sel, q, jnp.zeros_like(q))
        s = _nt_dot(qh, k) + bias_ref[hh]
        s = jnp.where(in_seq, s, NEG)
        m = jnp.max(s, axis=-1, keepdims=True)
        p = jnp.exp(s - m)
        l = jnp.sum(p, axis=-1, keepdims=True)
        pv = _dot(p.astype(BF16), v)
        out = out + jnp.where(sel, pv * (1.0 / l), 0.0)
    o_ref[...] = out


def _chunk_attn(pa, rb_row):
    b, s, _ = pa.shape
    tq = TQ_A
    blk = lambda colblk, back: pl.BlockSpec(
        (None, tq, D_BRANCH), lambda bi, i: (bi, jnp.maximum(i - back, 0), colblk))
    return pl.pallas_call(
        _chunk_attn_kernel,
        grid=(b, s // tq),
        in_specs=[blk(0, 0), blk(1, 2), blk(1, 1), blk(1, 0), blk(2, 2), blk(2, 1), blk(2, 0),
                  pl.BlockSpec(rb_row.shape, lambda bi, i: (0, 0))],
        out_specs=pl.BlockSpec((None, tq, D_BRANCH), lambda bi, i: (bi, i, 0)),
        out_shape=jax.ShapeDtypeStruct((b, s, D_BRANCH), F32),
        scratch_shapes=[pltpu.VMEM((N_HEADS, tq, WIN_A), F32)],
        compiler_params=pltpu.CompilerParams(
            dimension_semantics=("arbitrary", "arbitrary"), vmem_limit_bytes=VMEM_LIMIT),
        name="chunk_attn",
    )(pa, pa, pa, pa, pa, pa, pa, rb_row)


def _chunk_bias_row(rel_bias):
    u = np.arange(ROLL_A)
    e = np.where(u < WIN_A, u, u - ROLL_A)
    rel = np.clip(LOOK_A - e, -MAX_REL, MAX_REL) + MAX_REL
    return rel_bias[:, rel].astype(F32)


def _fox_kernel(cend_ref, q_ref, k_ref, vt_ref, cq_ref, o_ref,
                sa_ref, sb_ref, ma_ref, mb_ref, m_ref, acc_ref, kmax_ref):
    bi, hi, i = pl.program_id(0), pl.program_id(1), pl.program_id(2)
    tq = q_ref.shape[0]
    s_len = k_ref.shape[0]
    q = q_ref[...]
    m_ref[...] = jnp.full_like(m_ref, NEG)
    acc_ref[...] = jnp.zeros_like(acc_ref)
    lane = lax.broadcasted_iota(jnp.int32, (1, HEAD_PAD), 1)

    @pl.when(i == 0)
    def _():
        def blk(r, mx):
            kf = k_ref[pl.ds(pl.multiple_of(r * NORM_ROWS, NORM_ROWS), NORM_ROWS), :].astype(F32)
            k2 = jnp.sum(jnp.where(lane < HEAD_DIM, kf * kf, 0.0), axis=1, keepdims=True)
            return jnp.maximum(mx, jnp.max(k2))
        kmax_ref[0] = jnp.sqrt(lax.fori_loop(0, s_len // NORM_ROWS, blk, jnp.float32(0.0)))

    qf = q.astype(F32)
    q2 = jnp.where(lane < HEAD_DIM, qf * qf, 0.0).astype(BF16)
    n2 = _nt_dot(jnp.ones((8, HEAD_PAD), BF16), q2)[0:1, :]
    reach = jnp.sqrt(n2) * (kmax_ref[0] * NORM_SLACK) + cq_ref[...]
    cbase = (bi * pl.num_programs(1) + hi) * pl.num_programs(2)

    def scores(j, s_ref, cm_ref, masked):
        ks = pl.multiple_of(j * tq, tq)
        st = _nt_dot(k_ref[pl.ds(ks, tq), :], q)
        if masked:
            kp = lax.broadcasted_iota(jnp.int32, (tq, tq), 0)
            qp = lax.broadcasted_iota(jnp.int32, (tq, tq), 1)
            st = jnp.where(kp <= qp, st, NEG)
        s_ref[...] = st
        cm_ref[...] = jnp.max(st, axis=0, keepdims=True)

    def consume(j, s_ref, cm_ref):
        ks = pl.multiple_of(j * tq, tq)
        vt = vt_ref[:, pl.ds(ks, tq)]
        m_prev = m_ref[...]
        m_new = jnp.maximum(m_prev, cm_ref[...])
        alpha = jnp.exp2(m_prev - m_new)
        pt = jnp.exp2(s_ref[...] - m_new).astype(BF16)
        acc_ref[...] = alpha * acc_ref[...] + _dot(vt, pt)
        m_ref[...] = m_new

    scores(i, sa_ref, ma_ref, True)
    n_pairs = i // 2

    def live(p, g):
        return g - cend_ref[cbase + jnp.maximum(i - 2 * p, 0)] >= -FOX_EXIT

    def cond(state):
        p, g = state
        return (p < n_pairs) & live(p, g)

    def pair(state):
        p, _ = state
        g = jnp.max(reach - m_ref[...])
        j = i - 2 * p
        scores(j - 1, sb_ref, mb_ref, False)
        consume(j, sa_ref, ma_ref)
        scores(j - 2, sa_ref, ma_ref, False)
        consume(j - 1, sb_ref, mb_ref)
        return p + 1, g

    p_end, g_end = lax.while_loop(cond, pair, (jnp.int32(0), jnp.float32(-NEG)))
    finished = (p_end == n_pairs) & live(p_end, g_end)

    @pl.when(finished & (i % 2 == 1))
    def _():
        scores(0, sb_ref, mb_ref, False)
        consume(1, sa_ref, ma_ref)
        consume(0, sb_ref, mb_ref)

    @pl.when(finished & (i % 2 == 0))
    def _():
        consume(0, sa_ref, ma_ref)

    acc = acc_ref[...]
    out_t = acc * (1.0 / acc[HEAD_DIM:HEAD_DIM + 1, :])
    d_idx = lax.broadcasted_iota(jnp.int32, out_t.shape, 0)
    o_ref[...] = jnp.where(d_idx < HEAD_DIM, out_t, 0.0).T


def _fox(q, k, vt, c_row):
    b, nh, s, _ = q.shape
    tq = TQ_C
    cend = c_row[:, :, 0, tq - 1::tq].reshape(-1)
    qblk = pl.BlockSpec((None, None, tq, HEAD_PAD), lambda bi, hi, i, ce: (bi, hi, i, 0))
    full = pl.BlockSpec((None, None, s, HEAD_PAD), lambda bi, hi, i, ce: (bi, hi, 0, 0))
    full_t = pl.BlockSpec((None, None, HEAD_PAD, s), lambda bi, hi, i, ce: (bi, hi, 0, 0))
    crow = pl.BlockSpec((None, None, 1, tq), lambda bi, hi, i, ce: (bi, hi, 0, i))
    return pl.pallas_call(
        _fox_kernel,
        grid_spec=pltpu.PrefetchScalarGridSpec(
            num_scalar_prefetch=1,
            grid=(b, nh, s // tq),
            in_specs=[qblk, full, full_t, crow],
            out_specs=pl.BlockSpec((None, tq, HEAD_PAD), lambda bi, hi, i, ce: (bi, i, hi)),
            scratch_shapes=[pltpu.VMEM((tq, tq), F32), pltpu.VMEM((tq, tq), F32),
                            pltpu.VMEM((1, tq), F32), pltpu.VMEM((1, tq), F32),
                            pltpu.VMEM((1, tq), F32), pltpu.VMEM((HEAD_PAD, tq), F32),
                            pltpu.SMEM((1,), F32)]),
        out_shape=jax.ShapeDtypeStruct((b, s, D_PAD), F32),
        compiler_params=pltpu.CompilerParams(
            dimension_semantics=("arbitrary", "arbitrary", "arbitrary"),
            vmem_limit_bytes=VMEM_LIMIT),
        name="fox",
    )(cend, q, k, vt, c_row)


def _sb_kernel(q_ref, k_ref, v_ref, o_ref, r_ref, acc_ref):
    i = pl.program_id(1)
    nh, t = q_ref.shape[0], q_ref.shape[1]
    row = lax.broadcasted_iota(jnp.int32, (t, t), 0)
    col = lax.broadcasted_iota(jnp.int32, (t, t), 1)
    tri = (row >= col).astype(BF16)
    tri2 = jnp.concatenate([tri, tri], axis=0)

    def tile(hh, kb, keep):
        ks = pl.multiple_of(kb * t, t)
        k = k_ref[hh, pl.ds(ks, t), :]
        v = v_ref[hh, pl.ds(ks, t), :]
        z = _nt_dot(q_ref[hh], k)
        sp = jnp.maximum(z, 0.0) + jnp.log2(1.0 + jnp.exp2(-jnp.abs(z)))
        if keep is not None:
            sp = jnp.where(keep, sp, 0.0)
        hi = sp.astype(BF16)
        lo = (sp - hi.astype(F32)).astype(BF16)
        suf = _dot(jnp.concatenate([hi, lo], axis=1), tri2)
        return z, suf, v

    keep_a = col < row
    keep_b = i > 0
    rmin = None
    for hh in range(nh):
        z_a, suf_a, v_a = tile(hh, i, keep_a)
        z_b, suf_b, v_b = tile(hh, jnp.maximum(i - 1, 0), keep_b)
        tot_a = suf_a[:, 0:1]
        a_a = jnp.where(keep_a, jnp.exp2(z_a - suf_a), 0.0)
        a_b = jnp.where(keep_b, jnp.exp2(z_b - suf_b - tot_a), 0.0)
        acc_ref[hh] = _dot(a_a.astype(BF16), v_a) + _dot(a_b.astype(BF16), v_b)
        r0 = tot_a + suf_b[:, 0:1]
        r_ref[hh] = r0
        rmin = jnp.min(r0) if rmin is None else jnp.minimum(rmin, jnp.min(r0))

    def cond(state):
        kb, rm = state
        return (kb >= 0) & (rm < SB_EXIT)

    def body(state):
        kb, _ = state
        rm = None
        for hh in range(nh):
            z, suf, v = tile(hh, kb, None)
            r = r_ref[hh]
            a = jnp.exp2(z - suf - r)
            acc_ref[hh] += _dot(a.astype(BF16), v)
            r = r + suf[:, 0:1]
            r_ref[hh] = r
            rm = jnp.min(r) if rm is None else jnp.minimum(rm, jnp.min(r))
        return kb - 1, rm

    lax.while_loop(cond, body, (i - 2, rmin))
    for hh in range(nh):
        o_ref[:, hh * HEAD_PAD:(hh + 1) * HEAD_PAD] = acc_ref[hh]


def _stick_breaking(q, k, v):
    b, nh, s, _ = q.shape
    tq = TQ_D
    qblk = pl.BlockSpec((None, nh, tq, HEAD_PAD), lambda bi, i: (bi, 0, i, 0))
    full = pl.BlockSpec((None, nh, s, HEAD_PAD), lambda bi, i: (bi, 0, 0, 0),
                        pipeline_mode=pl.Buffered(1))
    return pl.pallas_call(
        _sb_kernel,
        grid=(b, s // tq),
        in_specs=[qblk, full, full],
        out_specs=pl.BlockSpec((None, tq, D_PAD), lambda bi, i: (bi, i, 0)),
        out_shape=jax.ShapeDtypeStruct((b, s, D_PAD), F32),
        scratch_shapes=[pltpu.VMEM((nh, tq, 1), F32), pltpu.VMEM((nh, tq, HEAD_PAD), F32)],
        compiler_params=pltpu.CompilerParams(
            dimension_semantics=("arbitrary", "arbitrary"),
            vmem_limit_bytes=VMEM_LIMIT),
        name="stick_breaking",
    )(q, k, v)


def _out_kernel(x_ref, ya_ref, ga_ref, pb_ref, yc_ref, gc_ref, yd_ref, gd_ref,
                bga_ref, bgb_ref, bgc_ref, bgd_ref, vg_ref, ws_ref, bs_ref,
                woa_ref, wob_ref, woc_ref, wod_ref, fg_ref, o_ref, *, last):
    tm = x_ref.shape[0]

    def gated(y, g_ref, gain_ref):
        ms = jnp.sum(y * y, axis=-1, keepdims=True) * (1.0 / D_BRANCH)
        g = g_ref[...].astype(F32)
        silu = g * (1.0 / (1.0 + jnp.exp(-g)))
        return (y * lax.rsqrt(ms + EPS) * gain_ref[...] * silu).astype(BF16)

    u = pb_ref[:, 0:D_BRANCH].astype(F32)
    vb = pb_ref[:, D_BRANCH:2 * D_BRANCH].astype(F32)
    mu = jnp.mean(vb, axis=-1, keepdims=True)
    xc = vb - mu
    var = jnp.mean(xc * xc, axis=-1, keepdims=True)
    vn = (xc * lax.rsqrt(var + EPS) * vg_ref[...]).astype(BF16)
    row = lax.broadcasted_iota(jnp.int32, (SG_CHUNK, SG_CHUNK), 0)
    col = lax.broadcasted_iota(jnp.int32, (SG_CHUNK, SG_CHUNK), 1)
    group_of_lane = lax.broadcasted_iota(jnp.int32, (1, D_BRANCH), 1) // HEAD_DIM
    ws = [jnp.where(row >= col, ws_ref[g], 0.0).astype(BF16) for g in range(N_HEADS)]
    mixed = []
    for ci in range(tm // SG_CHUNK):
        vchunk = vn[ci * SG_CHUNK:(ci + 1) * SG_CHUNK]
        acc = bs_ref[...]
        for g in range(N_HEADS):
            acc = acc + jnp.where(group_of_lane == g, _dot(ws[g], vchunk), 0.0)
        mixed.append(acc)
    yb = u * jnp.concatenate(mixed, axis=0)

    out = x_ref[...]
    out = out + _dot(gated(ya_ref[...], ga_ref, bga_ref), woa_ref[...])
    out = out + _dot(gated(yb, pb_ref.at[:, 2 * D_BRANCH:3 * D_BRANCH], bgb_ref), wob_ref[...])
    lane = lax.broadcasted_iota(jnp.int32, (1, HEAD_PAD), 1)

    def unpad(y_ref):
        halves = []
        for pr in range(N_HEADS // 2):
            even = y_ref[:, (2 * pr) * HEAD_PAD:(2 * pr + 1) * HEAD_PAD]
            odd = y_ref[:, (2 * pr + 1) * HEAD_PAD:(2 * pr + 2) * HEAD_PAD]
            halves.append(jnp.where(lane < HEAD_DIM, even, pltpu.roll(odd, HEAD_DIM, 1)))
        return jnp.concatenate(halves, axis=1)

    out = out + _dot(gated(unpad(yc_ref), gc_ref, bgc_ref), woc_ref[...])
    out = out + _dot(gated(unpad(yd_ref), gd_ref, bgd_ref), wod_ref[...])
    if last:
        ms = jnp.mean(out * out, axis=-1, keepdims=True)
        out = out * lax.rsqrt(ms + EPS) * fg_ref[...]
    o_ref[...] = out


def _out_proj(x, ya, pa, pb, yc, gc, yd, gd, bga, bgb, bgc, bgd, vg, ws, bs,
              woa, wob, woc, wod, fg, last):
    b, s, d = x.shape
    tm = TM_OUT
    tok = lambda n, colblk=0: pl.BlockSpec((None, tm, n), lambda bi, i: (bi, i, colblk))
    const = lambda a: pl.BlockSpec(a.shape, lambda bi, i: (0,) * a.ndim)
    return pl.pallas_call(
        functools.partial(_out_kernel, last=last),
        grid=(b, s // tm),
        in_specs=[tok(d), tok(D_BRANCH), tok(D_BRANCH, 3), tok(3 * D_BRANCH), tok(D_PAD), tok(D_BRANCH),
                  tok(D_PAD), tok(D_BRANCH), const(bga), const(bgb), const(bgc), const(bgd), const(vg),
                  const(ws), const(bs), const(woa), const(wob), const(woc), const(wod), const(fg)],
        out_specs=tok(d),
        out_shape=jax.ShapeDtypeStruct((b, s, d), F32),
        compiler_params=pltpu.CompilerParams(
            dimension_semantics=("arbitrary", "arbitrary"), vmem_limit_bytes=VMEM_LIMIT),
        name="out_proj",
    )(x, ya, pa, pb, yc, gc, yd, gd, bga, bgb, bgc, bgd, vg, ws, bs, woa, wob, woc, wod, fg)


def _layer(x, norm_g, w_in, b_f, rel_bias, w_s, b_s, v_gain, branch_gain, w_out, final_g, last):
    db = D_BRANCH
    o = 0
    wa = w_in[:, o:o + 4 * db]; o += 4 * db
    wb = w_in[:, o:o + 3 * db]; o += 3 * db
    wc = w_in[:, o:o + 4 * db]; o += 4 * db
    wf = w_in[:, o:o + N_HEADS]; o += N_HEADS
    wd = w_in[:, o:o + 4 * db]

    def q_scaled(w):
        return jnp.concatenate([w[:, :db] * SCALE, w[:, db:]], axis=1).astype(BF16)

    wf_p = jnp.pad(wf, ((0, 0), (0, HEAD_PAD - N_HEADS))).astype(BF16)
    bf_p = jnp.pad(b_f, (0, HEAD_PAD - N_HEADS)).reshape(1, HEAD_PAD)

    pa, pb, qc, kc, vc, gc, c2, qd, kd, vd, gd = _inproj(
        x, norm_g.reshape(1, -1), q_scaled(wa), wb.astype(BF16), q_scaled(wc), wf_p, q_scaled(wd), bf_p)

    ya = _chunk_attn(pa, _chunk_bias_row(rel_bias))
    c_row = jnp.transpose(c2[:, :, :N_HEADS], (0, 2, 1))[:, :, None, :]
    yc = _fox(qc, kc, jnp.swapaxes(vc, 2, 3), c_row)
    yd = _stick_breaking(qd, kd, vd)

    bs_tile = jnp.repeat(jnp.transpose(b_s), HEAD_DIM, axis=1)
    return _out_proj(
        x, ya, pa, pb, yc, gc, yd, gd,
        branch_gain[0].reshape(1, -1), branch_gain[1].reshape(1, -1),
        branch_gain[2].reshape(1, -1), branch_gain[3].reshape(1, -1),
        v_gain.reshape(1, -1), w_s, bs_tile,
        w_out[0:db].astype(BF16), w_out[db:2 * db].astype(BF16),
        w_out[2 * db:3 * db].astype(BF16), w_out[3 * db:4 * db].astype(BF16),
        final_g.reshape(1, -1), last)


def kernel(x, norm_g, w_in, b_f, rel_bias, w_s, b_s, v_gain, branch_gain, w_out, final_g):
    depth = norm_g.shape[0]
    for l in range(depth):
        x = _layer(x, norm_g[l], w_in[l], b_f[l], rel_bias[l], w_s[l], b_s[l], v_gain[l],
                   branch_gain[l], w_out[l], final_g, last=(l == depth - 1))
    return x
```

```python
import functools

import jax
import jax.numpy as jnp
import numpy as np
from jax import lax
from jax.experimental import pallas as pl
from jax.experimental.pallas import tpu as pltpu

D_MODEL = 1024
N_HEADS = 4
HEAD_DIM = 64
D_BRANCH = N_HEADS * HEAD_DIM
CHUNK = 64
LOOKBACK_CHUNKS = 8
MAX_REL = 128
SG_CHUNK = 128
EPS = 1e-6
SCALE = HEAD_DIM ** -0.5
HEAD_PAD = 128
D_PAD = N_HEADS * HEAD_PAD
NEG = -1e30
LOG2E = 1.4426950408889634
SB_EXIT = 151.0
FOX_EXIT = 152.0
NORM_SLACK = 1.01
NORM_ROWS = 2048

F32 = jnp.float32
BF16 = jnp.bfloat16

TM_IN = 512
TQ_A = 256
LOOK_A = LOOKBACK_CHUNKS * CHUNK
WIN_A = LOOK_A + TQ_A
ROLL_A = 1024
TQ_C = 512
QSPLIT_C = 256
TQ_D = 256
TM_OUT = 256
VMEM_LIMIT = 56 * 1024 * 1024


def _nt_dot(a, b):
    return lax.dot_general(a, b, (((1,), (1,)), ((), ())), preferred_element_type=F32)


def _dot(a, b):
    return jnp.dot(a, b, preferred_element_type=F32)


def _bf16_part(x):
    return x.astype(BF16).astype(F32)


def _split3(x):
    hi = _bf16_part(x)
    mid = _bf16_part(x - hi)
    return hi, mid, x - hi - mid


def _inproj_kernel(x_ref, g_ref, wa_ref, wb_ref, wc_ref, wf_ref, wd_ref, bf_ref,
                   pa_ref, pb_ref, qc_ref, kc_ref, vc_ref, gc_ref, c_ref,
                   qd_ref, kd_ref, vd_ref, gd_ref, carry_ref):
    i = pl.program_id(1)
    tm = x_ref.shape[0]
    x = x_ref[...]
    ms = jnp.mean(x * x, axis=-1, keepdims=True)
    h = (x * lax.rsqrt(ms + EPS) * g_ref[...]).astype(BF16)

    pa = _dot(h, wa_ref[...])
    pa_ref[:, :D_BRANCH] = (pa[:, :D_BRANCH] * LOG2E).astype(BF16)
    pa_ref[:, D_BRANCH:] = pa[:, D_BRANCH:].astype(BF16)
    pb_ref[...] = _dot(h, wb_ref[...]).astype(BF16)

    f = _dot(h, wf_ref[...]) + bf_ref[...]
    ls = jnp.minimum(f, 0.0) - jnp.log(1.0 + jnp.exp(-jnp.abs(f)))
    row = lax.broadcasted_iota(jnp.int32, (tm, tm), 0)
    col = lax.broadcasted_iota(jnp.int32, (tm, tm), 1)
    tri = (row >= col).astype(BF16)
    ls_hi, ls_mid, ls_lo = _split3(ls)
    parts = _dot(tri, jnp.concatenate([ls_hi, ls_mid, ls_lo], axis=1).astype(BF16))
    local = parts[:, :HEAD_PAD] + parts[:, HEAD_PAD:2 * HEAD_PAD] + parts[:, 2 * HEAD_PAD:]

    @pl.when(i == 0)
    def _():
        carry_ref[...] = jnp.zeros_like(carry_ref)

    c = local + carry_ref[...]
    carry_ref[...] = c[tm - 1:tm, :]
    c2 = c * LOG2E
    c_ref[...] = c2

    lane = lax.broadcasted_iota(jnp.int32, (1, HEAD_PAD), 1)
    a0 = HEAD_DIM
    k_const = ((lane >= a0) & (lane < a0 + 3)).astype(F32)
    q_const = -((lane >= a0 + 3) & (lane < a0 + 6)).astype(F32)
    v_const = (lane == a0).astype(F32)

    def head_tile(p, part, hh, spare, scale=None):
        lo = part * D_BRANCH + (hh // 2) * HEAD_PAD
        tile = p[:, lo:lo + HEAD_PAD]
        if hh % 2:
            tile = pltpu.roll(tile, HEAD_DIM, 1)
        if scale is not None:
            tile = tile * scale
        return jnp.where(lane < HEAD_DIM, tile, spare).astype(BF16)

    pc = _dot(h, wc_ref[...])
    pieces = _split3(c2)
    for hh in range(N_HEADS):
        hi, mid, lo = (jnp.broadcast_to(p[:, hh:hh + 1], (tm, HEAD_PAD)) for p in pieces)
        q_spare = jnp.where(lane == a0, hi, jnp.where(lane == a0 + 1, mid, jnp.where(lane == a0 + 2, lo, q_const)))
        k_spare = jnp.where(lane == a0 + 3, hi, jnp.where(lane == a0 + 4, mid, jnp.where(lane == a0 + 5, lo, k_const)))
        qc_ref[hh] = head_tile(pc, 0, hh, q_spare, LOG2E)
        kc_ref[hh] = head_tile(pc, 1, hh, k_spare)
        vc_ref[hh] = head_tile(pc, 2, hh, v_const)
    gc_ref[...] = pc[:, 3 * D_BRANCH:4 * D_BRANCH].astype(BF16)

    pd = _dot(h, wd_ref[...])
    for hh in range(N_HEADS):
        qd_ref[hh] = head_tile(pd, 0, hh, 0.0, LOG2E)
        kd_ref[hh] = head_tile(pd, 1, hh, 0.0)
        vd_ref[hh] = head_tile(pd, 2, hh, 0.0)
    gd_ref[...] = pd[:, 3 * D_BRANCH:4 * D_BRANCH].astype(BF16)


def _inproj(x, norm_g, wa, wb, wc, wf, wd, bf):
    b, s, d = x.shape
    tm = TM_IN
    const = lambda shape: pl.BlockSpec(shape, lambda bi, i: (0,) * len(shape))
    tok = lambda n: pl.BlockSpec((None, tm, n), lambda bi, i: (bi, i, 0))
    head = pl.BlockSpec((None, N_HEADS, tm, HEAD_PAD), lambda bi, i: (bi, 0, i, 0))
    head_shape = jax.ShapeDtypeStruct((b, N_HEADS, s, HEAD_PAD), BF16)
    return pl.pallas_call(
        _inproj_kernel,
        grid=(b, s // tm),
        in_specs=[tok(d), const((1, d)), const(wa.shape), const(wb.shape), const(wc.shape),
                  const(wf.shape), const(wd.shape), const((1, HEAD_PAD))],
        out_specs=[tok(wa.shape[1]), tok(wb.shape[1]), head, head, head, tok(D_BRANCH), tok(HEAD_PAD),
                   head, head, head, tok(D_BRANCH)],
        out_shape=[jax.ShapeDtypeStruct((b, s, wa.shape[1]), BF16),
                   jax.ShapeDtypeStruct((b, s, wb.shape[1]), BF16),
                   head_shape, head_shape, head_shape,
                   jax.ShapeDtypeStruct((b, s, D_BRANCH), BF16),
                   jax.ShapeDtypeStruct((b, s, HEAD_PAD), F32),
                   head_shape, head_shape, head_shape,
                   jax.ShapeDtypeStruct((b, s, D_BRANCH), BF16)],
        scratch_shapes=[pltpu.VMEM((1, HEAD_PAD), F32)],
        compiler_params=pltpu.CompilerParams(
            dimension_semantics=("arbitrary", "arbitrary"), vmem_limit_bytes=VMEM_LIMIT),
        name="inproj",
    )(x, norm_g, wa, wb, wc, wf, wd, bf)


def _chunk_attn_kernel(q_ref, k0_ref, k1_ref, k2_ref, v0_ref, v1_ref, v2_ref, rb_ref, o_ref, bias_ref):
    i = pl.program_id(1)
    tq = q_ref.shape[0]

    @pl.when((pl.program_id(0) == 0) & (i == 0))
    def _():
        ti = lax.broadcasted_iota(jnp.int32, (tq, WIN_A), 0) // CHUNK
        sj = lax.broadcasted_iota(jnp.int32, (tq, WIN_A), 1) // CHUNK
        band = (sj >= ti) & (sj <= ti + LOOKBACK_CHUNKS)
        for hh in range(N_HEADS):
            rows = jnp.broadcast_to(rb_ref[hh:hh + 1, :], (tq, ROLL_A))
            toep = pltpu.roll(rows, 0, 1, stride=1, stride_axis=0)
            bias_ref[hh] = jnp.where(band, toep[:, :WIN_A], NEG)

    q = q_ref[...]
    k = jnp.concatenate([k0_ref[...], k1_ref[...], k2_ref[...]], axis=0)
    v = jnp.concatenate([v0_ref[...], v1_ref[...], v2_ref[...]], axis=0)
    head_of_lane = lax.broadcasted_iota(jnp.int32, (1, D_BRANCH), 1) // HEAD_DIM
    col = lax.broadcasted_iota(jnp.int32, (tq, WIN_A), 1)
    in_seq = col >= LOOK_A - i * tq
    out = jnp.zeros((tq, D_BRANCH), F32)
    for hh in range(N_HEADS):
        sel = head_of_lane == hh
        qh = jnp.where(sel, q, jnp.zeros_like(q))
        s = _nt_dot(qh, k) + bias_ref[hh]
        s = jnp.where(in_seq, s, NEG)
        m = jnp.max(s, axis=-1, keepdims=True)
        p = jnp.exp2(s - m)
        l = jnp.sum(p, axis=-1, keepdims=True)
        pv = _dot(p.astype(BF16), v)
        out = out + jnp.where(sel, pv * (1.0 / l), 0.0)
    o_ref[...] = out.astype(o_ref.dtype)


def _chunk_attn(pa, rb_row):
    b, s, _ = pa.shape
    tq = TQ_A
    blk = lambda colblk, back: pl.BlockSpec(
        (None, tq, D_BRANCH), lambda bi, i: (bi, jnp.maximum(i - back, 0), colblk))
    return pl.pallas_call(
        _chunk_attn_kernel,
        grid=(b, s // tq),
        in_specs=[blk(0, 0), blk(1, 2), blk(1, 1), blk(1, 0), blk(2, 2), blk(2, 1), blk(2, 0),
                  pl.BlockSpec(rb_row.shape, lambda bi, i: (0, 0))],
        out_specs=pl.BlockSpec((None, tq, D_BRANCH), lambda bi, i: (bi, i, 0)),
        out_shape=jax.ShapeDtypeStruct((b, s, D_BRANCH), BF16),
        scratch_shapes=[pltpu.VMEM((N_HEADS, tq, WIN_A), F32)],
        compiler_params=pltpu.CompilerParams(
            dimension_semantics=("arbitrary", "arbitrary"), vmem_limit_bytes=VMEM_LIMIT),
        name="chunk_attn",
    )(pa, pa, pa, pa, pa, pa, pa, rb_row)


def _chunk_bias_row(rel_bias):
    u = np.arange(ROLL_A)
    e = np.where(u < WIN_A, u, u - ROLL_A)
    rel = np.clip(LOOK_A - e, -MAX_REL, MAX_REL) + MAX_REL
    return rel_bias[:, rel].astype(F32) * LOG2E


def _fox_kernel(cend_ref, q_ref, k_ref, vt_ref, cq_ref, o_ref,
                sa_ref, sb_ref, ma_ref, mb_ref, m_ref, acc_ref, kmax_ref):
    bi, hi, i = pl.program_id(0), pl.program_id(1), pl.program_id(2)
    tq = q_ref.shape[0]
    s_len = k_ref.shape[0]
    q = q_ref[...]
    m_ref[...] = jnp.full_like(m_ref, NEG)
    acc_ref[...] = jnp.zeros_like(acc_ref)
    lane = lax.broadcasted_iota(jnp.int32, (1, HEAD_PAD), 1)

    @pl.when(i == 0)
    def _():
        def blk(r, mx):
            kf = k_ref[pl.ds(pl.multiple_of(r * NORM_ROWS, NORM_ROWS), NORM_ROWS), :].astype(F32)
            k2 = jnp.sum(jnp.where(lane < HEAD_DIM, kf * kf, 0.0), axis=1, keepdims=True)
            return jnp.maximum(mx, jnp.max(k2))
        kmax_ref[0] = jnp.sqrt(lax.fori_loop(0, s_len // NORM_ROWS, blk, jnp.float32(0.0)))

    qf = q.astype(F32)
    q2 = jnp.where(lane < HEAD_DIM, qf * qf, 0.0).astype(BF16)
    n2 = _nt_dot(jnp.ones((8, HEAD_PAD), BF16), q2)[0:1, :]
    reach = jnp.sqrt(n2) * (kmax_ref[0] * NORM_SLACK) + cq_ref[...]
    cbase = (bi * pl.num_programs(1) + hi) * pl.num_programs(2)

    def scores(j, s_ref, cm_ref, masked):
        ks = pl.multiple_of(j * tq, tq)
        st = _nt_dot(k_ref[pl.ds(ks, tq), :], q)
        if masked:
            kp = lax.broadcasted_iota(jnp.int32, (tq, tq), 0)
            qp = lax.broadcasted_iota(jnp.int32, (tq, tq), 1)
            st = jnp.where(kp <= qp, st, NEG)
        s_ref[...] = st
        cm_ref[...] = jnp.max(st, axis=0, keepdims=True)

    def consume(j, s_ref, cm_ref):
        ks = pl.multiple_of(j * tq, tq)
        vt = vt_ref[:, pl.ds(ks, tq)]
        m_prev = m_ref[...]
        m_new = jnp.maximum(m_prev, cm_ref[...])
        alpha = jnp.exp2(m_prev - m_new)
        pt = jnp.exp2(s_ref[...] - m_new).astype(BF16)
        acc_ref[...] = alpha * acc_ref[...] + _dot(vt, pt)
        m_ref[...] = m_new

    scores(i, sa_ref, ma_ref, True)
    n_pairs = i // 2

    def live(p, g):
        return g - cend_ref[cbase + jnp.maximum(i - 2 * p, 0)] >= -FOX_EXIT

    def cond(state):
        p, g = state
        return (p < n_pairs) & live(p, g)

    def pair(state):
        p, _ = state
        g = jnp.max(reach - m_ref[...])
        j = i - 2 * p
        scores(j - 1, sb_ref, mb_ref, False)
        consume(j, sa_ref, ma_ref)
        scores(j - 2, sa_ref, ma_ref, False)
        consume(j - 1, sb_ref, mb_ref)
        return p + 1, g

    p_end, g_end = lax.while_loop(cond, pair, (jnp.int32(0), jnp.float32(-NEG)))
    finished = (p_end == n_pairs) & live(p_end, g_end)

    @pl.when(finished & (i % 2 == 1))
    def _():
        scores(0, sb_ref, mb_ref, False)
        consume(1, sa_ref, ma_ref)
        consume(0, sb_ref, mb_ref)

    @pl.when(finished & (i % 2 == 0))
    def _():
        consume(0, sa_ref, ma_ref)

    acc = acc_ref[...]
    out_t = acc * (1.0 / acc[HEAD_DIM:HEAD_DIM + 1, :])
    d_idx = lax.broadcasted_iota(jnp.int32, out_t.shape, 0)
    o_ref[...] = jnp.where(d_idx < HEAD_DIM, out_t, 0.0).T.astype(o_ref.dtype)


def _fox(q, k, vt, c_row):
    b, nh, s, _ = q.shape
    tq = TQ_C
    cend = c_row[:, :, 0, tq - 1::tq].reshape(-1)
    qblk = pl.BlockSpec((None, None, tq, HEAD_PAD), lambda bi, hi, i, ce: (bi, hi, i, 0))
    full = pl.BlockSpec((None, None, s, HEAD_PAD), lambda bi, hi, i, ce: (bi, hi, 0, 0))
    full_t = pl.BlockSpec((None, None, HEAD_PAD, s), lambda bi, hi, i, ce: (bi, hi, 0, 0))
    crow = pl.BlockSpec((None, None, 1, tq), lambda bi, hi, i, ce: (bi, hi, 0, i))
    return pl.pallas_call(
        _fox_kernel,
        grid_spec=pltpu.PrefetchScalarGridSpec(
            num_scalar_prefetch=1,
            grid=(b, nh, s // tq),
            in_specs=[qblk, full, full_t, crow],
            out_specs=pl.BlockSpec((None, tq, HEAD_PAD), lambda bi, hi, i, ce: (bi, i, hi)),
            scratch_shapes=[pltpu.VMEM((tq, tq), F32), pltpu.VMEM((tq, tq), F32),
                            pltpu.VMEM((1, tq), F32), pltpu.VMEM((1, tq), F32),
                            pltpu.VMEM((1, tq), F32), pltpu.VMEM((HEAD_PAD, tq), F32),
                            pltpu.SMEM((1,), F32)]),
        out_shape=jax.ShapeDtypeStruct((b, s, D_PAD), BF16),
        compiler_params=pltpu.CompilerParams(
            dimension_semantics=("arbitrary", "arbitrary", "arbitrary"),
            vmem_limit_bytes=VMEM_LIMIT),
        name="fox",
    )(cend, q, k, vt, c_row)


def _sb_kernel(q_ref, k_ref, v_ref, o_ref, r_ref, acc_ref):
    i = pl.program_id(1)
    nh, t = q_ref.shape[0], q_ref.shape[1]
    row = lax.broadcasted_iota(jnp.int32, (t, t), 0)
    col = lax.broadcasted_iota(jnp.int32, (t, t), 1)
    tri = (row >= col).astype(BF16)
    tri2 = jnp.concatenate([tri, tri], axis=0)

    def tile(hh, kb, keep):
        ks = pl.multiple_of(kb * t, t)
        k = k_ref[hh, pl.ds(ks, t), :]
        v = v_ref[hh, pl.ds(ks, t), :]
        z = _nt_dot(q_ref[hh], k)
        sp = jnp.maximum(z, 0.0) + jnp.log2(1.0 + jnp.exp2(-jnp.abs(z)))
        if keep is not None:
            sp = jnp.where(keep, sp, 0.0)
        hi = sp.astype(BF16)
        lo = (sp - hi.astype(F32)).astype(BF16)
        suf = _dot(jnp.concatenate([hi, lo], axis=1), tri2)
        return z, suf, v

    keep_a = col < row
    keep_b = i > 0
    rmin = None
    for hh in range(nh):
        z_a, suf_a, v_a = tile(hh, i, keep_a)
        z_b, suf_b, v_b = tile(hh, jnp.maximum(i - 1, 0), keep_b)
        tot_a = suf_a[:, 0:1]
        a_a = jnp.where(keep_a, jnp.exp2(z_a - suf_a), 0.0)
        a_b = jnp.where(keep_b, jnp.exp2(z_b - suf_b - tot_a), 0.0)
        acc_ref[hh] = _dot(a_a.astype(BF16), v_a) + _dot(a_b.astype(BF16), v_b)
        r0 = tot_a + suf_b[:, 0:1]
        r_ref[hh] = r0
        rmin = jnp.min(r0) if rmin is None else jnp.minimum(rmin, jnp.min(r0))

    def cond(state):
        kb, rm = state
        return (kb >= 0) & (rm < SB_EXIT)

    def body(state):
        kb, _ = state
        rm = None
        for hh in range(nh):
            z, suf, v = tile(hh, kb, None)
            r = r_ref[hh]
            a = jnp.exp2(z - suf - r)
            acc_ref[hh] += _dot(a.astype(BF16), v)
            r = r + suf[:, 0:1]
            r_ref[hh] = r
            rm = jnp.min(r) if rm is None else jnp.minimum(rm, jnp.min(r))
        return kb - 1, rm

    lax.while_loop(cond, body, (i - 2, rmin))
    for hh in range(nh):
        o_ref[:, hh * HEAD_PAD:(hh + 1) * HEAD_PAD] = acc_ref[hh].astype(o_ref.dtype)


def _stick_breaking(q, k, v):
    b, nh, s, _ = q.shape
    tq = TQ_D
    qblk = pl.BlockSpec((None, nh, tq, HEAD_PAD), lambda bi, i: (bi, 0, i, 0))
    full = pl.BlockSpec((None, nh, s, HEAD_PAD), lambda bi, i: (bi, 0, 0, 0),
                        pipeline_mode=pl.Buffered(1))
    return pl.pallas_call(
        _sb_kernel,
        grid=(b, s // tq),
        in_specs=[qblk, full, full],
        out_specs=pl.BlockSpec((None, tq, D_PAD), lambda bi, i: (bi, i, 0)),
        out_shape=jax.ShapeDtypeStruct((b, s, D_PAD), BF16),
        scratch_shapes=[pltpu.VMEM((nh, tq, 1), F32), pltpu.VMEM((nh, tq, HEAD_PAD), F32)],
        compiler_params=pltpu.CompilerParams(
            dimension_semantics=("arbitrary", "arbitrary"),
            vmem_limit_bytes=VMEM_LIMIT),
        name="stick_breaking",
    )(q, k, v)


def _out_kernel(x_ref, ya_ref, ga_ref, pb_ref, yc_ref, gc_ref, yd_ref, gd_ref,
                bga_ref, bgb_ref, bgc_ref, bgd_ref, vg_ref, ws_ref, bs_ref,
                woa_ref, wob_ref, woc_ref, wod_ref, fg_ref, o_ref, *, last):
    tm = x_ref.shape[0]

    def gated(y, g_ref, gain_ref):
        ms = jnp.sum(y * y, axis=-1, keepdims=True) * (1.0 / D_BRANCH)
        g = g_ref[...].astype(F32)
        silu = g * (1.0 / (1.0 + jnp.exp(-g)))
        return (y * lax.rsqrt(ms + EPS) * gain_ref[...] * silu).astype(BF16)

    u = pb_ref[:, 0:D_BRANCH].astype(F32)
    vb = pb_ref[:, D_BRANCH:2 * D_BRANCH].astype(F32)
    mu = jnp.mean(vb, axis=-1, keepdims=True)
    xc = vb - mu
    var = jnp.mean(xc * xc, axis=-1, keepdims=True)
    vn = (xc * lax.rsqrt(var + EPS) * vg_ref[...]).astype(BF16)
    row = lax.broadcasted_iota(jnp.int32, (SG_CHUNK, SG_CHUNK), 0)
    col = lax.broadcasted_iota(jnp.int32, (SG_CHUNK, SG_CHUNK), 1)
    group_of_lane = lax.broadcasted_iota(jnp.int32, (1, D_BRANCH), 1) // HEAD_DIM
    ws = [jnp.where(row >= col, ws_ref[g], 0.0).astype(BF16) for g in range(N_HEADS)]
    mixed = []
    for ci in range(tm // SG_CHUNK):
        vchunk = vn[ci * SG_CHUNK:(ci + 1) * SG_CHUNK]
        acc = bs_ref[...]
        for g in range(N_HEADS):
            acc = acc + jnp.where(group_of_lane == g, _dot(ws[g], vchunk), 0.0)
        mixed.append(acc)
    yb = u * jnp.concatenate(mixed, axis=0)

    out = x_ref[...]
    out = out + _dot(gated(ya_ref[...].astype(F32), ga_ref, bga_ref), woa_ref[...])
    out = out + _dot(gated(yb, pb_ref.at[:, 2 * D_BRANCH:3 * D_BRANCH], bgb_ref), wob_ref[...])
    lane = lax.broadcasted_iota(jnp.int32, (1, HEAD_PAD), 1)

    def unpad(y_ref):
        halves = []
        for pr in range(N_HEADS // 2):
            even = y_ref[:, (2 * pr) * HEAD_PAD:(2 * pr + 1) * HEAD_PAD].astype(F32)
            odd = y_ref[:, (2 * pr + 1) * HEAD_PAD:(2 * pr + 2) * HEAD_PAD].astype(F32)
            halves.append(jnp.where(lane < HEAD_DIM, even, pltpu.roll(odd, HEAD_DIM, 1)))
        return jnp.concatenate(halves, axis=1)

    out = out + _dot(gated(unpad(yc_ref), gc_ref, bgc_ref), woc_ref[...])
    out = out + _dot(gated(unpad(yd_ref), gd_ref, bgd_ref), wod_ref[...])
    if last:
        ms = jnp.mean(out * out, axis=-1, keepdims=True)
        out = out * lax.rsqrt(ms + EPS) * fg_ref[...]
    o_ref[...] = out


def _out_proj(x, ya, pa, pb, yc, gc, yd, gd, bga, bgb, bgc, bgd, vg, ws, bs,
              woa, wob, woc, wod, fg, last):
    b, s, d = x.shape
    tm = TM_OUT
    tok = lambda n, colblk=0: pl.BlockSpec((None, tm, n), lambda bi, i: (bi, i, colblk))
    const = lambda a: pl.BlockSpec(a.shape, lambda bi, i: (0,) * a.ndim)
    return pl.pallas_call(
        functools.partial(_out_kernel, last=last),
        grid=(b, s // tm),
        in_specs=[tok(d), tok(D_BRANCH), tok(D_BRANCH, 3), tok(3 * D_BRANCH), tok(D_PAD), tok(D_BRANCH),
                  tok(D_PAD), tok(D_BRANCH), const(bga), const(bgb), const(bgc), const(bgd), const(vg),
                  const(ws), const(bs), const(woa), const(wob), const(woc), const(wod), const(fg)],
        out_specs=tok(d),
        out_shape=jax.ShapeDtypeStruct((b, s, d), F32),
        compiler_params=pltpu.CompilerParams(
            dimension_semantics=("arbitrary", "arbitrary"), vmem_limit_bytes=VMEM_LIMIT),
        name="out_proj",
    )(x, ya, pa, pb, yc, gc, yd, gd, bga, bgb, bgc, bgd, vg, ws, bs, woa, wob, woc, wod, fg)


def _layer(x, norm_g, w_in, b_f, rel_bias, w_s, b_s, v_gain, branch_gain, w_out, final_g, last):
    db = D_BRANCH
    o = 0
    wa = w_in[:, o:o + 4 * db]; o += 4 * db
    wb = w_in[:, o:o + 3 * db]; o += 3 * db
    wc = w_in[:, o:o + 4 * db]; o += 4 * db
    wf = w_in[:, o:o + N_HEADS]; o += N_HEADS
    wd = w_in[:, o:o + 4 * db]

    def q_scaled(w):
        return jnp.concatenate([w[:, :db] * SCALE, w[:, db:]], axis=1).astype(BF16)

    wf_p = jnp.pad(wf, ((0, 0), (0, HEAD_PAD - N_HEADS))).astype(BF16)
    bf_p = jnp.pad(b_f, (0, HEAD_PAD - N_HEADS)).reshape(1, HEAD_PAD)

    pa, pb, qc, kc, vc, gc, c2, qd, kd, vd, gd = _inproj(
        x, norm_g.reshape(1, -1), q_scaled(wa), wb.astype(BF16), q_scaled(wc), wf_p, q_scaled(wd), bf_p)

    ya = _chunk_attn(pa, _chunk_bias_row(rel_bias))
    c_row = jnp.transpose(c2[:, :, :N_HEADS], (0, 2, 1))[:, :, None, :]
    yc = _fox(qc, kc, jnp.swapaxes(vc, 2, 3), c_row)
    yd = _stick_breaking(qd, kd, vd)

    bs_tile = jnp.repeat(jnp.transpose(b_s), HEAD_DIM, axis=1)
    return _out_proj(
        x, ya, pa, pb, yc, gc, yd, gd,
        branch_gain[0].reshape(1, -1), branch_gain[1].reshape(1, -1),
        branch_gain[2].reshape(1, -1), branch_gain[3].reshape(1, -1),
        v_gain.reshape(1, -1), w_s, bs_tile,
        w_out[0:db].astype(BF16), w_out[db:2 * db].astype(BF16),
        w_out[2 * db:3 * db].astype(BF16), w_out[3 * db:4 * db].astype(BF16),
        final_g.reshape(1, -1), last)


def kernel(x, norm_g, w_in, b_f, rel_bias, w_s, b_s, v_gain, branch_gain, w_out, final_g):
    depth = norm_g.shape[0]
    for l in range(depth):
        x = _layer(x, norm_g[l], w_in[l], b_f[l], rel_bias[l], w_s[l], b_s[l], v_gain[l],
                   branch_gain[l], w_out[l], final_g, last=(l == depth - 1))
    return x
```

```python
import functools

import jax
import jax.numpy as jnp
import numpy as np
from jax import lax
from jax.experimental import pallas as pl
from jax.experimental.pallas import tpu as pltpu

D_MODEL = 1024
N_HEADS = 4
HEAD_DIM = 64
D_BRANCH = N_HEADS * HEAD_DIM
CHUNK = 64
LOOKBACK_CHUNKS = 8
MAX_REL = 128
SG_CHUNK = 128
EPS = 1e-6
SCALE = HEAD_DIM ** -0.5
HEAD_PAD = 128
D_PAD = N_HEADS * HEAD_PAD
NEG = -1e30
LOG2E = 1.4426950408889634
SB_EXIT = 151.0
FOX_EXIT = 152.0
NORM_SLACK = 1.01
NORM_ROWS = 2048

F32 = jnp.float32
BF16 = jnp.bfloat16

TM_IN = 512
TQ_A = 256
LOOK_A = LOOKBACK_CHUNKS * CHUNK
WIN_A = LOOK_A + TQ_A
ROLL_A = 1024
TQ_C = 512
QSPLIT_C = 256
TQ_D = 256
TM_OUT = 256
VMEM_LIMIT = 56 * 1024 * 1024


def _nt_dot(a, b):
    return lax.dot_general(a, b, (((1,), (1,)), ((), ())), preferred_element_type=F32)


def _dot(a, b):
    return jnp.dot(a, b, preferred_element_type=F32)


def _bf16_part(x):
    return x.astype(BF16).astype(F32)


def _split3(x):
    hi = _bf16_part(x)
    mid = _bf16_part(x - hi)
    return hi, mid, x - hi - mid


def _inproj_kernel(x_ref, g_ref, wa_ref, wb_ref, wc_ref, wf_ref, wd_ref, bf_ref,
                   pa_ref, pb_ref, qc_ref, kc_ref, vc_ref, gc_ref, c_ref,
                   qd_ref, kd_ref, vd_ref, gd_ref, carry_ref):
    i = pl.program_id(1)
    tm = x_ref.shape[0]
    x = x_ref[...]
    ms = jnp.mean(x * x, axis=-1, keepdims=True)
    h = (x * lax.rsqrt(ms + EPS) * g_ref[...]).astype(BF16)

    pa = _dot(h, wa_ref[...])
    pa_ref[:, :D_BRANCH] = (pa[:, :D_BRANCH] * LOG2E).astype(BF16)
    pa_ref[:, D_BRANCH:] = pa[:, D_BRANCH:].astype(BF16)
    pb_ref[...] = _dot(h, wb_ref[...]).astype(BF16)

    f = _dot(h, wf_ref[...]) + bf_ref[...]
    ls = jnp.minimum(f, 0.0) - jnp.log(1.0 + jnp.exp(-jnp.abs(f)))
    row = lax.broadcasted_iota(jnp.int32, (tm, tm), 0)
    col = lax.broadcasted_iota(jnp.int32, (tm, tm), 1)
    tri = (row >= col).astype(BF16)
    ls_hi, ls_mid, ls_lo = _split3(ls)
    parts = _dot(tri, jnp.concatenate([ls_hi, ls_mid, ls_lo], axis=1).astype(BF16))
    local = parts[:, :HEAD_PAD] + parts[:, HEAD_PAD:2 * HEAD_PAD] + parts[:, 2 * HEAD_PAD:]

    @pl.when(i == 0)
    def _():
        carry_ref[...] = jnp.zeros_like(carry_ref)

    c = local + carry_ref[...]
    carry_ref[...] = c[tm - 1:tm, :]
    c2 = c * LOG2E
    c_ref[...] = c2

    lane = lax.broadcasted_iota(jnp.int32, (1, HEAD_PAD), 1)
    a0 = HEAD_DIM
    k_const = ((lane >= a0) & (lane < a0 + 3)).astype(F32)
    q_const = -((lane >= a0 + 3) & (lane < a0 + 6)).astype(F32)
    v_const = (lane == a0).astype(F32)

    def head_tile(p, part, hh, spare, scale=None):
        lo = part * D_BRANCH + (hh // 2) * HEAD_PAD
        tile = p[:, lo:lo + HEAD_PAD]
        if hh % 2:
            tile = pltpu.roll(tile, HEAD_DIM, 1)
        if scale is not None:
            tile = tile * scale
        return jnp.where(lane < HEAD_DIM, tile, spare).astype(BF16)

    pc = _dot(h, wc_ref[...])
    pieces = _split3(c2)
    for hh in range(N_HEADS):
        hi, mid, lo = (jnp.broadcast_to(p[:, hh:hh + 1], (tm, HEAD_PAD)) for p in pieces)
        q_spare = jnp.where(lane == a0, hi, jnp.where(lane == a0 + 1, mid, jnp.where(lane == a0 + 2, lo, q_const)))
        k_spare = jnp.where(lane == a0 + 3, hi, jnp.where(lane == a0 + 4, mid, jnp.where(lane == a0 + 5, lo, k_const)))
        qc_ref[hh] = head_tile(pc, 0, hh, q_spare, LOG2E)
        kc_ref[hh] = head_tile(pc, 1, hh, k_spare)
        vc_ref[hh] = head_tile(pc, 2, hh, v_const)
    gc_ref[...] = pc[:, 3 * D_BRANCH:4 * D_BRANCH].astype(BF16)

    pd = _dot(h, wd_ref[...])
    for hh in range(N_HEADS):
        qd_ref[hh] = head_tile(pd, 0, hh, 0.0, LOG2E)
        kd_ref[hh] = head_tile(pd, 1, hh, 0.0)
        vd_ref[hh] = head_tile(pd, 2, hh, 0.0)
    gd_ref[...] = pd[:, 3 * D_BRANCH:4 * D_BRANCH].astype(BF16)


def _inproj(x, norm_g, wa, wb, wc, wf, wd, bf):
    b, s, d = x.shape
    tm = TM_IN
    const = lambda shape: pl.BlockSpec(shape, lambda bi, i: (0,) * len(shape))
    tok = lambda n: pl.BlockSpec((None, tm, n), lambda bi, i: (bi, i, 0))
    head = pl.BlockSpec((None, N_HEADS, tm, HEAD_PAD), lambda bi, i: (bi, 0, i, 0))
    head_shape = jax.ShapeDtypeStruct((b, N_HEADS, s, HEAD_PAD), BF16)
    return pl.pallas_call(
        _inproj_kernel,
        grid=(b, s // tm),
        in_specs=[tok(d), const((1, d)), const(wa.shape), const(wb.shape), const(wc.shape),
                  const(wf.shape), const(wd.shape), const((1, HEAD_PAD))],
        out_specs=[tok(wa.shape[1]), tok(wb.shape[1]), head, head, head, tok(D_BRANCH), tok(HEAD_PAD),
                   head, head, head, tok(D_BRANCH)],
        out_shape=[jax.ShapeDtypeStruct((b, s, wa.shape[1]), BF16),
                   jax.ShapeDtypeStruct((b, s, wb.shape[1]), BF16),
                   head_shape, head_shape, head_shape,
                   jax.ShapeDtypeStruct((b, s, D_BRANCH), BF16),
                   jax.ShapeDtypeStruct((b, s, HEAD_PAD), F32),
                   head_shape, head_shape, head_shape,
                   jax.ShapeDtypeStruct((b, s, D_BRANCH), BF16)],
        scratch_shapes=[pltpu.VMEM((1, HEAD_PAD), F32)],
        compiler_params=pltpu.CompilerParams(
            dimension_semantics=("arbitrary", "arbitrary"), vmem_limit_bytes=VMEM_LIMIT),
        name="inproj",
    )(x, norm_g, wa, wb, wc, wf, wd, bf)


def _chunk_attn_kernel(q_ref, k0_ref, k1_ref, k2_ref, v0_ref, v1_ref, v2_ref, rb_ref, o_ref, bias_ref):
    i = pl.program_id(1)
    tq = q_ref.shape[0]

    @pl.when((pl.program_id(0) == 0) & (i == 0))
    def _():
        ti = lax.broadcasted_iota(jnp.int32, (tq, WIN_A), 0) // CHUNK
        sj = lax.broadcasted_iota(jnp.int32, (tq, WIN_A), 1) // CHUNK
        band = (sj >= ti) & (sj <= ti + LOOKBACK_CHUNKS)
        for hh in range(N_HEADS):
            rows = jnp.broadcast_to(rb_ref[hh:hh + 1, :], (tq, ROLL_A))
            toep = pltpu.roll(rows, 0, 1, stride=1, stride_axis=0)
            bias_ref[hh] = jnp.where(band, toep[:, :WIN_A], NEG)

    q = q_ref[...]
    k = jnp.concatenate([k0_ref[...], k1_ref[...], k2_ref[...]], axis=0)
    v = jnp.concatenate([v0_ref[...], v1_ref[...], v2_ref[...]], axis=0)
    head_of_lane = lax.broadcasted_iota(jnp.int32, (1, D_BRANCH), 1) // HEAD_DIM
    col = lax.broadcasted_iota(jnp.int32, (tq, WIN_A), 1)
    in_seq = col >= LOOK_A - i * tq
    out = jnp.zeros((tq, D_BRANCH), F32)
    for hh in range(N_HEADS):
        sel = head_of_lane == hh
        qh = jnp.where(sel, q, jnp.zeros_like(q))
        s = _nt_dot(qh, k) + bias_ref[hh]
        s = jnp.where(in_seq, s, NEG)
        m = jnp.max(s, axis=-1, keepdims=True)
        p = jnp.exp2(s - m)
        l = jnp.sum(p, axis=-1, keepdims=True)
        pv = _dot(p.astype(BF16), v)
        out = out + jnp.where(sel, pv * (1.0 / l), 0.0)
    o_ref[...] = out.astype(o_ref.dtype)


def _chunk_attn(pa, rb_row):
    b, s, _ = pa.shape
    tq = TQ_A
    blk = lambda colblk, back: pl.BlockSpec(
        (None, tq, D_BRANCH), lambda bi, i: (bi, jnp.maximum(i - back, 0), colblk))
    return pl.pallas_call(
        _chunk_attn_kernel,
        grid=(b, s // tq),
        in_specs=[blk(0, 0), blk(1, 2), blk(1, 1), blk(1, 0), blk(2, 2), blk(2, 1), blk(2, 0),
                  pl.BlockSpec(rb_row.shape, lambda bi, i: (0, 0))],
        out_specs=pl.BlockSpec((None, tq, D_BRANCH), lambda bi, i: (bi, i, 0)),
        out_shape=jax.ShapeDtypeStruct((b, s, D_BRANCH), BF16),
        scratch_shapes=[pltpu.VMEM((N_HEADS, tq, WIN_A), F32)],
        compiler_params=pltpu.CompilerParams(
            dimension_semantics=("arbitrary", "arbitrary"), vmem_limit_bytes=VMEM_LIMIT),
        name="chunk_attn",
    )(pa, pa, pa, pa, pa, pa, pa, rb_row)


def _chunk_bias_row(rel_bias):
    u = np.arange(ROLL_A)
    e = np.where(u < WIN_A, u, u - ROLL_A)
    rel = np.clip(LOOK_A - e, -MAX_REL, MAX_REL) + MAX_REL
    return rel_bias[:, rel].astype(F32) * LOG2E


def _fox_kernel(cend_ref, q_ref, k_ref, vt_ref, c_ref, o_ref,
                sa_ref, sb_ref, ma_ref, mb_ref, m_ref, acc_ref, reach_ref, cap_ref):
    bi, hi = pl.program_id(0), pl.program_id(1)
    tq = sa_ref.shape[0]
    s_len = k_ref.shape[0]
    n_tiles = s_len // tq
    cbase = (bi * pl.num_programs(1) + hi) * n_tiles
    lane = lax.broadcasted_iota(jnp.int32, (1, HEAD_PAD), 1)
    big = jnp.float32(-NEG)

    def key_blk(r, mx):
        kf = k_ref[pl.ds(pl.multiple_of(r * NORM_ROWS, NORM_ROWS), NORM_ROWS), :].astype(F32)
        k2 = jnp.sum(jnp.where(lane < HEAD_DIM, kf * kf, 0.0), axis=1, keepdims=True)
        return jnp.maximum(mx, jnp.max(k2))

    kmax = jnp.sqrt(lax.fori_loop(0, s_len // NORM_ROWS, key_blk, jnp.float32(0.0)))

    def reach_blk(r, carry):
        rows = pl.ds(pl.multiple_of(r * NORM_ROWS, NORM_ROWS), NORM_ROWS)
        qf = q_ref[rows, :].astype(F32)
        q2 = jnp.where(lane < HEAD_DIM, qf * qf, 0.0).astype(BF16)
        n2 = _nt_dot(jnp.ones((8, HEAD_PAD), BF16), q2)[0:1, :]
        reach_ref[:, rows] = jnp.sqrt(n2) * (kmax * NORM_SLACK) + c_ref[:, rows]
        return carry

    lax.fori_loop(0, s_len // NORM_ROWS, reach_blk, 0)
    causal = (lax.broadcasted_iota(jnp.int32, (tq, tq), 0)
              <= lax.broadcasted_iota(jnp.int32, (tq, tq), 1))
    cap_ref[0] = jnp.full((tq, tq), -NEG, F32)
    cap_ref[1] = jnp.where(causal, -NEG, NEG)
    acc_ref[...] = jnp.zeros_like(acc_ref)

    def prefetch(pi, pj, s_ref, cm_ref):
        pi = jnp.minimum(pi, n_tiles - 1)
        pj = jnp.minimum(pj, n_tiles - 1)
        q = q_ref[pl.ds(pl.multiple_of(pi * tq, tq), tq), :]
        k = k_ref[pl.ds(pl.multiple_of(pj * tq, tq), tq), :]
        st = _nt_dot(k, q)
        st = jnp.minimum(st, cap_ref[(pi == pj).astype(jnp.int32)])
        s_ref[...] = st
        cm_ref[...] = jnp.max(st, axis=0, keepdims=True)

    def consume(ti, tj, s_ref, cm_ref):
        tc = jnp.minimum(ti, n_tiles - 1)
        cols = pl.ds(pl.multiple_of(jnp.minimum(tj, n_tiles - 1) * tq, tq), tq)
        par = ti % 2
        m_prev = jnp.where(tj == ti, NEG, m_ref[...])
        m_new = jnp.maximum(m_prev, cm_ref[...])
        alpha = jnp.exp2(m_prev - m_new)
        pt = jnp.exp2(s_ref[...] - m_new).astype(BF16)
        acc_ref[par] = alpha * acc_ref[par] + _dot(vt_ref[:, cols], pt)
        m_ref[...] = m_new
        return jnp.max(reach_ref[:, pl.ds(pl.multiple_of(tc * tq, tq), tq)] - m_new)

    def successor(ti, tj, g):
        g = jnp.where(tj == ti, big, g)
        cont = (tj >= 1) & (g - cend_ref[cbase + jnp.clip(tj - 1, 0, n_tiles - 1)] >= -FOX_EXIT)
        return jnp.where(cont, ti, ti + 1), jnp.where(cont, tj - 1, ti + 1)

    def finalize(t):
        acc = acc_ref[t % 2]
        out_t = acc * (1.0 / acc[HEAD_DIM:HEAD_DIM + 1, :])
        d_idx = lax.broadcasted_iota(jnp.int32, out_t.shape, 0)
        rows = pl.ds(pl.multiple_of(t * tq, tq), tq)
        o_ref[rows, :] = jnp.where(d_idx < HEAD_DIM, out_t, 0.0).T.astype(o_ref.dtype)

    prefetch(0, 0, sa_ref, ma_ref)

    def cond(state):
        return state[0] < n_tiles

    def body(state):
        ti, tj, g, pend, pend_tile = state

        @pl.when(pend == 1)
        def _():
            finalize(pend_tile)

        ui, uj = successor(ti, tj, g)
        prefetch(ui, uj, sb_ref, mb_ref)
        ga = consume(ti, tj, sa_ref, ma_ref)
        wi, wj = successor(ui, uj, ga)
        prefetch(wi, wj, sa_ref, ma_ref)
        gb = consume(ui, uj, sb_ref, mb_ref)
        ends_b = (wi != ui) & (ui < n_tiles)
        ended = (ui != ti) | ends_b
        return wi, wj, gb, ended.astype(jnp.int32), jnp.where(ends_b, ui, ti)

    state = lax.while_loop(cond, body, (jnp.int32(0), jnp.int32(0), big, jnp.int32(0), jnp.int32(0)))

    @pl.when(state[3] == 1)
    def _():
        finalize(state[4])


def _fox(q, k, vt, c_row):
    b, nh, s, _ = q.shape
    tq = TQ_C
    cend = c_row[:, :, 0, tq - 1::tq].reshape(-1)
    full = pl.BlockSpec((None, None, s, HEAD_PAD), lambda bi, hi, ce: (bi, hi, 0, 0))
    full_t = pl.BlockSpec((None, None, HEAD_PAD, s), lambda bi, hi, ce: (bi, hi, 0, 0))
    crow = pl.BlockSpec((None, None, 1, s), lambda bi, hi, ce: (bi, hi, 0, 0))
    return pl.pallas_call(
        _fox_kernel,
        grid_spec=pltpu.PrefetchScalarGridSpec(
            num_scalar_prefetch=1,
            grid=(b, nh),
            in_specs=[full, full, full_t, crow],
            out_specs=pl.BlockSpec((None, s, HEAD_PAD), lambda bi, hi, ce: (bi, 0, hi)),
            scratch_shapes=[pltpu.VMEM((tq, tq), F32), pltpu.VMEM((tq, tq), F32),
                            pltpu.VMEM((1, tq), F32), pltpu.VMEM((1, tq), F32),
                            pltpu.VMEM((1, tq), F32), pltpu.VMEM((2, HEAD_PAD, tq), F32),
                            pltpu.VMEM((1, s), F32), pltpu.VMEM((2, tq, tq), F32)]),
        out_shape=jax.ShapeDtypeStruct((b, s, D_PAD), BF16),
        compiler_params=pltpu.CompilerParams(
            dimension_semantics=("arbitrary", "arbitrary"),
            vmem_limit_bytes=VMEM_LIMIT),
        name="fox",
    )(cend, q, k, vt, c_row)


def _sb_kernel(q_ref, k_ref, v_ref, o_ref, r_ref, acc_ref):
    i = pl.program_id(1)
    nh, t = q_ref.shape[0], q_ref.shape[1]
    row = lax.broadcasted_iota(jnp.int32, (t, t), 0)
    col = lax.broadcasted_iota(jnp.int32, (t, t), 1)
    tri = (row >= col).astype(BF16)
    tri2 = jnp.concatenate([tri, tri], axis=0)

    def tile(hh, kb, keep):
        ks = pl.multiple_of(kb * t, t)
        k = k_ref[hh, pl.ds(ks, t), :]
        v = v_ref[hh, pl.ds(ks, t), :]
        z = _nt_dot(q_ref[hh], k)
        sp = jnp.maximum(z, 0.0) + jnp.log2(1.0 + jnp.exp2(-jnp.abs(z)))
        if keep is not None:
            sp = jnp.where(keep, sp, 0.0)
        hi = sp.astype(BF16)
        lo = (sp - hi.astype(F32)).astype(BF16)
        suf = _dot(jnp.concatenate([hi, lo], axis=1), tri2)
        return z, suf, v

    keep_a = col < row
    keep_b = i > 0
    rmin = None
    for hh in range(nh):
        z_a, suf_a, v_a = tile(hh, i, keep_a)
        z_b, suf_b, v_b = tile(hh, jnp.maximum(i - 1, 0), keep_b)
        tot_a = suf_a[:, 0:1]
        a_a = jnp.where(keep_a, jnp.exp2(z_a - suf_a), 0.0)
        a_b = jnp.where(keep_b, jnp.exp2(z_b - suf_b - tot_a), 0.0)
        acc_ref[hh] = _dot(a_a.astype(BF16), v_a) + _dot(a_b.astype(BF16), v_b)
        r0 = tot_a + suf_b[:, 0:1]
        r_ref[hh] = r0
        rmin = jnp.min(r0) if rmin is None else jnp.minimum(rmin, jnp.min(r0))

    def cond(state):
        kb, rm = state
        return (kb >= 0) & (rm < SB_EXIT)

    def body(state):
        kb, _ = state
        rm = None
        for hh in range(nh):
            z, suf, v = tile(hh, kb, None)
            r = r_ref[hh]
            a = jnp.exp2(z - suf - r)
            acc_ref[hh] += _dot(a.astype(BF16), v)
            r = r + suf[:, 0:1]
            r_ref[hh] = r
            rm = jnp.min(r) if rm is None else jnp.minimum(rm, jnp.min(r))
        return kb - 1, rm

    lax.while_loop(cond, body, (i - 2, rmin))
    for hh in range(nh):
        o_ref[:, hh * HEAD_PAD:(hh + 1) * HEAD_PAD] = acc_ref[hh].astype(o_ref.dtype)


def _stick_breaking(q, k, v):
    b, nh, s, _ = q.shape
    tq = TQ_D
    qblk = pl.BlockSpec((None, nh, tq, HEAD_PAD), lambda bi, i: (bi, 0, i, 0))
    full = pl.BlockSpec((None, nh, s, HEAD_PAD), lambda bi, i: (bi, 0, 0, 0),
                        pipeline_mode=pl.Buffered(1))
    return pl.pallas_call(
        _sb_kernel,
        grid=(b, s // tq),
        in_specs=[qblk, full, full],
        out_specs=pl.BlockSpec((None, tq, D_PAD), lambda bi, i: (bi, i, 0)),
        out_shape=jax.ShapeDtypeStruct((b, s, D_PAD), BF16),
        scratch_shapes=[pltpu.VMEM((nh, tq, 1), F32), pltpu.VMEM((nh, tq, HEAD_PAD), F32)],
        compiler_params=pltpu.CompilerParams(
            dimension_semantics=("arbitrary", "arbitrary"),
            vmem_limit_bytes=VMEM_LIMIT),
        name="stick_breaking",
    )(q, k, v)


def _out_kernel(x_ref, ya_ref, ga_ref, pb_ref, yc_ref, gc_ref, yd_ref, gd_ref,
                bga_ref, bgb_ref, bgc_ref, bgd_ref, vg_ref, ws_ref, bs_ref,
                woa_ref, wob_ref, woc_ref, wod_ref, fg_ref, o_ref, *, last):
    tm = x_ref.shape[0]

    def gated(y, g_ref, gain_ref):
        ms = jnp.sum(y * y, axis=-1, keepdims=True) * (1.0 / D_BRANCH)
        g = g_ref[...].astype(F32)
        silu = g * (1.0 / (1.0 + jnp.exp(-g)))
        return (y * lax.rsqrt(ms + EPS) * gain_ref[...] * silu).astype(BF16)

    u = pb_ref[:, 0:D_BRANCH].astype(F32)
    vb = pb_ref[:, D_BRANCH:2 * D_BRANCH].astype(F32)
    mu = jnp.mean(vb, axis=-1, keepdims=True)
    xc = vb - mu
    var = jnp.mean(xc * xc, axis=-1, keepdims=True)
    vn = (xc * lax.rsqrt(var + EPS) * vg_ref[...]).astype(BF16)
    row = lax.broadcasted_iota(jnp.int32, (SG_CHUNK, SG_CHUNK), 0)
    col = lax.broadcasted_iota(jnp.int32, (SG_CHUNK, SG_CHUNK), 1)
    group_of_lane = lax.broadcasted_iota(jnp.int32, (1, D_BRANCH), 1) // HEAD_DIM
    ws = [jnp.where(row >= col, ws_ref[g], 0.0).astype(BF16) for g in range(N_HEADS)]
    mixed = []
    for ci in range(tm // SG_CHUNK):
        vchunk = vn[ci * SG_CHUNK:(ci + 1) * SG_CHUNK]
        acc = bs_ref[...]
        for g in range(N_HEADS):
            acc = acc + jnp.where(group_of_lane == g, _dot(ws[g], vchunk), 0.0)
        mixed.append(acc)
    yb = u * jnp.concatenate(mixed, axis=0)

    out = x_ref[...]
    out = out + _dot(gated(ya_ref[...].astype(F32), ga_ref, bga_ref), woa_ref[...])
    out = out + _dot(gated(yb, pb_ref.at[:, 2 * D_BRANCH:3 * D_BRANCH], bgb_ref), wob_ref[...])
    lane = lax.broadcasted_iota(jnp.int32, (1, HEAD_PAD), 1)

    def unpad(y_ref):
        halves = []
        for pr in range(N_HEADS // 2):
            even = y_ref[:, (2 * pr) * HEAD_PAD:(2 * pr + 1) * HEAD_PAD].astype(F32)
            odd = y_ref[:, (2 * pr + 1) * HEAD_PAD:(2 * pr + 2) * HEAD_PAD].astype(F32)
            halves.append(jnp.where(lane < HEAD_DIM, even, pltpu.roll(odd, HEAD_DIM, 1)))
        return jnp.concatenate(halves, axis=1)

    out = out + _dot(gated(unpad(yc_ref), gc_ref, bgc_ref), woc_ref[...])
    out = out + _dot(gated(unpad(yd_ref), gd_ref, bgd_ref), wod_ref[...])
    if last:
        ms = jnp.mean(out * out, axis=-1, keepdims=True)
        out = out * lax.rsqrt(ms + EPS) * fg_ref[...]
    o_ref[...] = out


def _out_proj(x, ya, pa, pb, yc, gc, yd, gd, bga, bgb, bgc, bgd, vg, ws, bs,
              woa, wob, woc, wod, fg, last):
    b, s, d = x.shape
    tm = TM_OUT
    tok = lambda n, colblk=0: pl.BlockSpec((None, tm, n), lambda bi, i: (bi, i, colblk))
    const = lambda a: pl.BlockSpec(a.shape, lambda bi, i: (0,) * a.ndim)
    return pl.pallas_call(
        functools.partial(_out_kernel, last=last),
        grid=(b, s // tm),
        in_specs=[tok(d), tok(D_BRANCH), tok(D_BRANCH, 3), tok(3 * D_BRANCH), tok(D_PAD), tok(D_BRANCH),
                  tok(D_PAD), tok(D_BRANCH), const(bga), const(bgb), const(bgc), const(bgd), const(vg),
                  const(ws), const(bs), const(woa), const(wob), const(woc), const(wod), const(fg)],
        out_specs=tok(d),
        out_shape=jax.ShapeDtypeStruct((b, s, d), F32),
        compiler_params=pltpu.CompilerParams(
            dimension_semantics=("arbitrary", "arbitrary"), vmem_limit_bytes=VMEM_LIMIT),
        name="out_proj",
    )(x, ya, pa, pb, yc, gc, yd, gd, bga, bgb, bgc, bgd, vg, ws, bs, woa, wob, woc, wod, fg)


def _layer(x, norm_g, w_in, b_f, rel_bias, w_s, b_s, v_gain, branch_gain, w_out, final_g, last):
    db = D_BRANCH
    o = 0
    wa = w_in[:, o:o + 4 * db]; o += 4 * db
    wb = w_in[:, o:o + 3 * db]; o += 3 * db
    wc = w_in[:, o:o + 4 * db]; o += 4 * db
    wf = w_in[:, o:o + N_HEADS]; o += N_HEADS
    wd = w_in[:, o:o + 4 * db]

    def q_scaled(w):
        return jnp.concatenate([w[:, :db] * SCALE, w[:, db:]], axis=1).astype(BF16)

    wf_p = jnp.pad(wf, ((0, 0), (0, HEAD_PAD - N_HEADS))).astype(BF16)
    bf_p = jnp.pad(b_f, (0, HEAD_PAD - N_HEADS)).reshape(1, HEAD_PAD)

    pa, pb, qc, kc, vc, gc, c2, qd, kd, vd, gd = _inproj(
        x, norm_g.reshape(1, -1), q_scaled(wa), wb.astype(BF16), q_scaled(wc), wf_p, q_scaled(wd), bf_p)

    ya = _chunk_attn(pa, _chunk_bias_row(rel_bias))
    c_row = jnp.transpose(c2[:, :, :N_HEADS], (0, 2, 1))[:, :, None, :]
    yc = _fox(qc, kc, jnp.swapaxes(vc, 2, 3), c_row)
    yd = _stick_breaking(qd, kd, vd)

    bs_tile = jnp.repeat(jnp.transpose(b_s), HEAD_DIM, axis=1)
    return _out_proj(
        x, ya, pa, pb, yc, gc, yd, gd,
        branch_gain[0].reshape(1, -1), branch_gain[1].reshape(1, -1),
        branch_gain[2].reshape(1, -1), branch_gain[3].reshape(1, -1),
        v_gain.reshape(1, -1), w_s, bs_tile,
        w_out[0:db].astype(BF16), w_out[db:2 * db].astype(BF16),
        w_out[2 * db:3 * db].astype(BF16), w_out[3 * db:4 * db].astype(BF16),
        final_g.reshape(1, -1), last)


def kernel(x, norm_g, w_in, b_f, rel_bias, w_s, b_s, v_gain, branch_gain, w_out, final_g):
    depth = norm_g.shape[0]
    for l in range(depth):
        x = _layer(x, norm_g[l], w_in[l], b_f[l], rel_bias[l], w_s[l], b_s[l], v_gain[l],
                   branch_gain[l], w_out[l], final_g, last=(l == depth - 1))
    return x
```

```python
import functools

import jax
import jax.numpy as jnp
import numpy as np
from jax import lax
from jax.experimental import pallas as pl
from jax.experimental.pallas import tpu as pltpu

D_MODEL = 1024
N_HEADS = 4
HEAD_DIM = 64
D_BRANCH = N_HEADS * HEAD_DIM
CHUNK = 64
LOOKBACK_CHUNKS = 8
MAX_REL = 128
SG_CHUNK = 128
EPS = 1e-6
SCALE = HEAD_DIM ** -0.5
HEAD_PAD = 128
D_PAD = N_HEADS * HEAD_PAD
NEG = -1e30
LOG2E = 1.4426950408889634
SB_EXIT = 151.0
FOX_EXIT = 152.0
NORM_SLACK = 1.01
NORM_ROWS = 2048

F32 = jnp.float32
BF16 = jnp.bfloat16

TM_IN = 512
TQ_A = 256
LOOK_A = LOOKBACK_CHUNKS * CHUNK
WIN_A = LOOK_A + TQ_A
ROLL_A = 1024
TQ_C = 1024
TK_C = 512
KPQ_C = TQ_C // TK_C
TQ_D = 256
TM_OUT = 256
VMEM_LIMIT = 56 * 1024 * 1024


def _nt_dot(a, b):
    return lax.dot_general(a, b, (((1,), (1,)), ((), ())), preferred_element_type=F32)


def _dot(a, b):
    return jnp.dot(a, b, preferred_element_type=F32)


def _bf16_part(x):
    return x.astype(BF16).astype(F32)


def _split3(x):
    hi = _bf16_part(x)
    mid = _bf16_part(x - hi)
    return hi, mid, x - hi - mid


def _inproj_kernel(x_ref, g_ref, wa_ref, wb_ref, wc_ref, wf_ref, wd_ref, bf_ref,
                   pa_ref, pb_ref, qc_ref, kc_ref, vc_ref, gc_ref, c_ref,
                   qd_ref, kd_ref, vd_ref, gd_ref, carry_ref):
    i = pl.program_id(1)
    tm = x_ref.shape[0]
    x = x_ref[...]
    ms = jnp.mean(x * x, axis=-1, keepdims=True)
    h = (x * lax.rsqrt(ms + EPS) * g_ref[...]).astype(BF16)

    pa = _dot(h, wa_ref[...])
    pa_ref[:, :D_BRANCH] = (pa[:, :D_BRANCH] * LOG2E).astype(BF16)
    pa_ref[:, D_BRANCH:] = pa[:, D_BRANCH:].astype(BF16)
    pb_ref[...] = _dot(h, wb_ref[...]).astype(BF16)

    f = _dot(h, wf_ref[...]) + bf_ref[...]
    ls = jnp.minimum(f, 0.0) - jnp.log(1.0 + jnp.exp(-jnp.abs(f)))
    row = lax.broadcasted_iota(jnp.int32, (tm, tm), 0)
    col = lax.broadcasted_iota(jnp.int32, (tm, tm), 1)
    tri = (row >= col).astype(BF16)
    ls_hi, ls_mid, ls_lo = _split3(ls)
    parts = _dot(tri, jnp.concatenate([ls_hi, ls_mid, ls_lo], axis=1).astype(BF16))
    local = parts[:, :HEAD_PAD] + parts[:, HEAD_PAD:2 * HEAD_PAD] + parts[:, 2 * HEAD_PAD:]

    @pl.when(i == 0)
    def _():
        carry_ref[...] = jnp.zeros_like(carry_ref)

    c = local + carry_ref[...]
    carry_ref[...] = c[tm - 1:tm, :]
    c2 = c * LOG2E
    c_ref[...] = c2

    lane = lax.broadcasted_iota(jnp.int32, (1, HEAD_PAD), 1)
    a0 = HEAD_DIM
    k_const = ((lane >= a0) & (lane < a0 + 3)).astype(F32)
    q_const = -((lane >= a0 + 3) & (lane < a0 + 6)).astype(F32)
    v_const = (lane == a0).astype(F32)

    def head_tile(p, part, hh, spare, scale=None):
        lo = part * D_BRANCH + (hh // 2) * HEAD_PAD
        tile = p[:, lo:lo + HEAD_PAD]
        if hh % 2:
            tile = pltpu.roll(tile, HEAD_DIM, 1)
        if scale is not None:
            tile = tile * scale
        return jnp.where(lane < HEAD_DIM, tile, spare).astype(BF16)

    pc = _dot(h, wc_ref[...])
    pieces = _split3(c2)
    for hh in range(N_HEADS):
        hi, mid, lo = (jnp.broadcast_to(p[:, hh:hh + 1], (tm, HEAD_PAD)) for p in pieces)
        q_spare = jnp.where(lane == a0, hi, jnp.where(lane == a0 + 1, mid, jnp.where(lane == a0 + 2, lo, q_const)))
        k_spare = jnp.where(lane == a0 + 3, hi, jnp.where(lane == a0 + 4, mid, jnp.where(lane == a0 + 5, lo, k_const)))
        qc_ref[hh] = head_tile(pc, 0, hh, q_spare, LOG2E)
        kc_ref[hh] = head_tile(pc, 1, hh, k_spare)
        vc_ref[hh] = head_tile(pc, 2, hh, v_const)
    gc_ref[...] = pc[:, 3 * D_BRANCH:4 * D_BRANCH].astype(BF16)

    pd = _dot(h, wd_ref[...])
    for hh in range(N_HEADS):
        qd_ref[hh] = head_tile(pd, 0, hh, 0.0, LOG2E)
        kd_ref[hh] = head_tile(pd, 1, hh, 0.0)
        vd_ref[hh] = head_tile(pd, 2, hh, 0.0)
    gd_ref[...] = pd[:, 3 * D_BRANCH:4 * D_BRANCH].astype(BF16)


def _inproj(x, norm_g, wa, wb, wc, wf, wd, bf):
    b, s, d = x.shape
    tm = TM_IN
    const = lambda shape: pl.BlockSpec(shape, lambda bi, i: (0,) * len(shape))
    tok = lambda n: pl.BlockSpec((None, tm, n), lambda bi, i: (bi, i, 0))
    head = pl.BlockSpec((None, N_HEADS, tm, HEAD_PAD), lambda bi, i: (bi, 0, i, 0))
    head_shape = jax.ShapeDtypeStruct((b, N_HEADS, s, HEAD_PAD), BF16)
    return pl.pallas_call(
        _inproj_kernel,
        grid=(b, s // tm),
        in_specs=[tok(d), const((1, d)), const(wa.shape), const(wb.shape), const(wc.shape),
                  const(wf.shape), const(wd.shape), const((1, HEAD_PAD))],
        out_specs=[tok(wa.shape[1]), tok(wb.shape[1]), head, head, head, tok(D_BRANCH), tok(HEAD_PAD),
                   head, head, head, tok(D_BRANCH)],
        out_shape=[jax.ShapeDtypeStruct((b, s, wa.shape[1]), BF16),
                   jax.ShapeDtypeStruct((b, s, wb.shape[1]), BF16),
                   head_shape, head_shape, head_shape,
                   jax.ShapeDtypeStruct((b, s, D_BRANCH), BF16),
                   jax.ShapeDtypeStruct((b, s, HEAD_PAD), F32),
                   head_shape, head_shape, head_shape,
                   jax.ShapeDtypeStruct((b, s, D_BRANCH), BF16)],
        scratch_shapes=[pltpu.VMEM((1, HEAD_PAD), F32)],
        compiler_params=pltpu.CompilerParams(
            dimension_semantics=("arbitrary", "arbitrary"), vmem_limit_bytes=VMEM_LIMIT),
        name="inproj",
    )(x, norm_g, wa, wb, wc, wf, wd, bf)


def _chunk_attn_kernel(q_ref, k0_ref, k1_ref, k2_ref, v0_ref, v1_ref, v2_ref, rb_ref, o_ref, bias_ref):
    i = pl.program_id(1)
    tq = q_ref.shape[0]

    @pl.when((pl.program_id(0) == 0) & (i == 0))
    def _():
        ti = lax.broadcasted_iota(jnp.int32, (tq, WIN_A), 0) // CHUNK
        sj = lax.broadcasted_iota(jnp.int32, (tq, WIN_A), 1) // CHUNK
        band = (sj >= ti) & (sj <= ti + LOOKBACK_CHUNKS)
        for hh in range(N_HEADS):
            rows = jnp.broadcast_to(rb_ref[hh:hh + 1, :], (tq, ROLL_A))
            toep = pltpu.roll(rows, 0, 1, stride=1, stride_axis=0)
            bias_ref[hh] = jnp.where(band, toep[:, :WIN_A], NEG)

    q = q_ref[...]
    k = jnp.concatenate([k0_ref[...], k1_ref[...], k2_ref[...]], axis=0)
    v = jnp.concatenate([v0_ref[...], v1_ref[...], v2_ref[...]], axis=0)
    head_of_lane = lax.broadcasted_iota(jnp.int32, (1, D_BRANCH), 1) // HEAD_DIM
    col = lax.broadcasted_iota(jnp.int32, (tq, WIN_A), 1)
    in_seq = col >= LOOK_A - i * tq
    out = jnp.zeros((tq, D_BRANCH), F32)
    for hh in range(N_HEADS):
        sel = head_of_lane == hh
        qh = jnp.where(sel, q, jnp.zeros_like(q))
        s = _nt_dot(qh, k) + bias_ref[hh]
        s = jnp.where(in_seq, s, NEG)
        m = jnp.max(s, axis=-1, keepdims=True)
        p = jnp.exp2(s - m)
        l = jnp.sum(p, axis=-1, keepdims=True)
        pv = _dot(p.astype(BF16), v)
        out = out + jnp.where(sel, pv * (1.0 / l), 0.0)
    o_ref[...] = out.astype(o_ref.dtype)


def _chunk_attn(pa, rb_row):
    b, s, _ = pa.shape
    tq = TQ_A
    blk = lambda colblk, back: pl.BlockSpec(
        (None, tq, D_BRANCH), lambda bi, i: (bi, jnp.maximum(i - back, 0), colblk))
    return pl.pallas_call(
        _chunk_attn_kernel,
        grid=(b, s // tq),
        in_specs=[blk(0, 0), blk(1, 2), blk(1, 1), blk(1, 0), blk(2, 2), blk(2, 1), blk(2, 0),
                  pl.BlockSpec(rb_row.shape, lambda bi, i: (0, 0))],
        out_specs=pl.BlockSpec((None, tq, D_BRANCH), lambda bi, i: (bi, i, 0)),
        out_shape=jax.ShapeDtypeStruct((b, s, D_BRANCH), BF16),
        scratch_shapes=[pltpu.VMEM((N_HEADS, tq, WIN_A), F32)],
        compiler_params=pltpu.CompilerParams(
            dimension_semantics=("arbitrary", "arbitrary"), vmem_limit_bytes=VMEM_LIMIT),
        name="chunk_attn",
    )(pa, pa, pa, pa, pa, pa, pa, rb_row)


def _chunk_bias_row(rel_bias):
    u = np.arange(ROLL_A)
    e = np.where(u < WIN_A, u, u - ROLL_A)
    rel = np.clip(LOOK_A - e, -MAX_REL, MAX_REL) + MAX_REL
    return rel_bias[:, rel].astype(F32) * LOG2E


def _fox_kernel(cend_ref, q_ref, k_ref, vt_ref, cq_ref, o_ref,
                sa_ref, sb_ref, ma_ref, mb_ref, m_ref, acc_ref, kmax_ref):
    bi, hi, i = pl.program_id(0), pl.program_id(1), pl.program_id(2)
    tq = q_ref.shape[0]
    tk = sa_ref.shape[0]
    s_len = k_ref.shape[0]
    q = q_ref[...]
    m_ref[...] = jnp.full_like(m_ref, NEG)
    acc_ref[...] = jnp.zeros_like(acc_ref)
    lane = lax.broadcasted_iota(jnp.int32, (1, HEAD_PAD), 1)

    @pl.when(i == 0)
    def _():
        def blk(r, mx):
            kf = k_ref[pl.ds(pl.multiple_of(r * NORM_ROWS, NORM_ROWS), NORM_ROWS), :].astype(F32)
            k2 = jnp.sum(jnp.where(lane < HEAD_DIM, kf * kf, 0.0), axis=1, keepdims=True)
            return jnp.maximum(mx, jnp.max(k2))
        kmax_ref[0] = jnp.sqrt(lax.fori_loop(0, s_len // NORM_ROWS, blk, jnp.float32(0.0)))

    qf = q.astype(F32)
    q2 = jnp.where(lane < HEAD_DIM, qf * qf, 0.0).astype(BF16)
    n2 = _nt_dot(jnp.ones((8, HEAD_PAD), BF16), q2)[0:1, :]
    reach = jnp.sqrt(n2) * (kmax_ref[0] * NORM_SLACK) + cq_ref[...]
    cbase = (bi * pl.num_programs(1) + hi) * (s_len // tk)

    def scores(j, s_ref, cm_ref, diag_off=None):
        ks = pl.multiple_of(jnp.maximum(j, 0) * tk, tk)
        st = _nt_dot(k_ref[pl.ds(ks, tk), :], q)
        if diag_off is not None:
            kp = lax.broadcasted_iota(jnp.int32, (tk, tq), 0) + diag_off
            qp = lax.broadcasted_iota(jnp.int32, (tk, tq), 1)
            st = jnp.where(kp <= qp, st, NEG)
        s_ref[...] = st
        cm_ref[...] = jnp.max(st, axis=0, keepdims=True)

    def consume(j, s_ref, cm_ref):
        ks = pl.multiple_of(j * tk, tk)
        vt = vt_ref[:, pl.ds(ks, tk)]
        m_prev = m_ref[...]
        m_new = jnp.maximum(m_prev, cm_ref[...])
        alpha = jnp.exp2(m_prev - m_new)
        pt = jnp.exp2(s_ref[...] - m_new).astype(BF16)
        acc_ref[...] = alpha * acc_ref[...] + _dot(vt, pt)
        m_ref[...] = m_new

    top = KPQ_C * i + 1
    scores(top, sa_ref, ma_ref, tk)
    scores(top - 1, sb_ref, mb_ref, 0)
    consume(top, sa_ref, ma_ref)
    scores(top - 2, sa_ref, ma_ref)
    consume(top - 1, sb_ref, mb_ref)

    def cond(state):
        p, g = state
        return (p <= i) & (g - cend_ref[cbase + jnp.maximum(top - 2 * p, 0)] >= -FOX_EXIT)

    def pair(state):
        p, _ = state
        g = jnp.max(reach - m_ref[...])
        j = top - 2 * p
        scores(j - 1, sb_ref, mb_ref)
        consume(j, sa_ref, ma_ref)
        scores(j - 2, sa_ref, ma_ref)
        consume(j - 1, sb_ref, mb_ref)
        return p + 1, g

    lax.while_loop(cond, pair, (jnp.int32(1), jnp.max(reach - m_ref[...])))

    acc = acc_ref[...]
    out_t = acc * (1.0 / acc[HEAD_DIM:HEAD_DIM + 1, :])
    d_idx = lax.broadcasted_iota(jnp.int32, out_t.shape, 0)
    o_ref[...] = jnp.where(d_idx < HEAD_DIM, out_t, 0.0).T.astype(o_ref.dtype)


def _fox(q, k, vt, c_row):
    b, nh, s, _ = q.shape
    tq, tk = TQ_C, TK_C
    cend = c_row[:, :, 0, tk - 1::tk].reshape(-1)
    qblk = pl.BlockSpec((None, None, tq, HEAD_PAD), lambda bi, hi, i, ce: (bi, hi, i, 0))
    full = pl.BlockSpec((None, None, s, HEAD_PAD), lambda bi, hi, i, ce: (bi, hi, 0, 0))
    full_t = pl.BlockSpec((None, None, HEAD_PAD, s), lambda bi, hi, i, ce: (bi, hi, 0, 0))
    crow = pl.BlockSpec((None, None, 1, tq), lambda bi, hi, i, ce: (bi, hi, 0, i))
    return pl.pallas_call(
        _fox_kernel,
        grid_spec=pltpu.PrefetchScalarGridSpec(
            num_scalar_prefetch=1,
            grid=(b, nh, s // tq),
            in_specs=[qblk, full, full_t, crow],
            out_specs=pl.BlockSpec((None, tq, HEAD_PAD), lambda bi, hi, i, ce: (bi, i, hi)),
            scratch_shapes=[pltpu.VMEM((tk, tq), F32), pltpu.VMEM((tk, tq), F32),
                            pltpu.VMEM((1, tq), F32), pltpu.VMEM((1, tq), F32),
                            pltpu.VMEM((1, tq), F32), pltpu.VMEM((HEAD_PAD, tq), F32),
                            pltpu.SMEM((1,), F32)]),
        out_shape=jax.ShapeDtypeStruct((b, s, D_PAD), BF16),
        compiler_params=pltpu.CompilerParams(
            dimension_semantics=("arbitrary", "arbitrary", "arbitrary"),
            vmem_limit_bytes=VMEM_LIMIT),
        name="fox",
    )(cend, q, k, vt, c_row)


def _sb_kernel(q_ref, k_ref, v_ref, o_ref, r_ref, acc_ref):
    i = pl.program_id(1)
    nh, t = q_ref.shape[0], q_ref.shape[1]
    row = lax.broadcasted_iota(jnp.int32, (t, t), 0)
    col = lax.broadcasted_iota(jnp.int32, (t, t), 1)
    tri = (row >= col).astype(BF16)
    tri2 = jnp.concatenate([tri, tri], axis=0)

    def tile(hh, kb, keep):
        ks = pl.multiple_of(kb * t, t)
        k = k_ref[hh, pl.ds(ks, t), :]
        v = v_ref[hh, pl.ds(ks, t), :]
        z = _nt_dot(q_ref[hh], k)
        sp = jnp.maximum(z, 0.0) + jnp.log2(1.0 + jnp.exp2(-jnp.abs(z)))
        if keep is not None:
            sp = jnp.where(keep, sp, 0.0)
        hi = sp.astype(BF16)
        lo = (sp - hi.astype(F32)).astype(BF16)
        suf = _dot(jnp.concatenate([hi, lo], axis=1), tri2)
        return z, suf, v

    keep_a = col < row
    keep_b = i > 0
    rmin = None
    for hh in range(nh):
        z_a, suf_a, v_a = tile(hh, i, keep_a)
        z_b, suf_b, v_b = tile(hh, jnp.maximum(i - 1, 0), keep_b)
        tot_a = suf_a[:, 0:1]
        a_a = jnp.where(keep_a, jnp.exp2(z_a - suf_a), 0.0)
        a_b = jnp.where(keep_b, jnp.exp2(z_b - suf_b - tot_a), 0.0)
        acc_ref[hh] = _dot(a_a.astype(BF16), v_a) + _dot(a_b.astype(BF16), v_b)
        r0 = tot_a + suf_b[:, 0:1]
        r_ref[hh] = r0
        rmin = jnp.min(r0) if rmin is None else jnp.minimum(rmin, jnp.min(r0))

    def cond(state):
        kb, rm = state
        return (kb >= 0) & (rm < SB_EXIT)

    def body(state):
        kb, _ = state
        rm = None
        for hh in range(nh):
            z, suf, v = tile(hh, kb, None)
            r = r_ref[hh]
            a = jnp.exp2(z - suf - r)
            acc_ref[hh] += _dot(a.astype(BF16), v)
            r = r + suf[:, 0:1]
            r_ref[hh] = r
            rm = jnp.min(r) if rm is None else jnp.minimum(rm, jnp.min(r))
        return kb - 1, rm

    lax.while_loop(cond, body, (i - 2, rmin))
    for hh in range(nh):
        o_ref[:, hh * HEAD_PAD:(hh + 1) * HEAD_PAD] = acc_ref[hh].astype(o_ref.dtype)


def _stick_breaking(q, k, v):
    b, nh, s, _ = q.shape
    tq = TQ_D
    qblk = pl.BlockSpec((None, nh, tq, HEAD_PAD), lambda bi, i: (bi, 0, i, 0))
    full = pl.BlockSpec((None, nh, s, HEAD_PAD), lambda bi, i: (bi, 0, 0, 0),
                        pipeline_mode=pl.Buffered(1))
    return pl.pallas_call(
        _sb_kernel,
        grid=(b, s // tq),
        in_specs=[qblk, full, full],
        out_specs=pl.BlockSpec((None, tq, D_PAD), lambda bi, i: (bi, i, 0)),
        out_shape=jax.ShapeDtypeStruct((b, s, D_PAD), BF16),
        scratch_shapes=[pltpu.VMEM((nh, tq, 1), F32), pltpu.VMEM((nh, tq, HEAD_PAD), F32)],
        compiler_params=pltpu.CompilerParams(
            dimension_semantics=("arbitrary", "arbitrary"),
            vmem_limit_bytes=VMEM_LIMIT),
        name="stick_breaking",
    )(q, k, v)


def _out_kernel(x_ref, ya_ref, ga_ref, pb_ref, yc_ref, gc_ref, yd_ref, gd_ref,
                bga_ref, bgb_ref, bgc_ref, bgd_ref, vg_ref, ws_ref, bs_ref,
                woa_ref, wob_ref, woc_ref, wod_ref, fg_ref, o_ref, *, last):
    tm = x_ref.shape[0]

    def gated(y, g_ref, gain_ref):
        ms = jnp.sum(y * y, axis=-1, keepdims=True) * (1.0 / D_BRANCH)
        g = g_ref[...].astype(F32)
        silu = g * (1.0 / (1.0 + jnp.exp(-g)))
        return (y * lax.rsqrt(ms + EPS) * gain_ref[...] * silu).astype(BF16)

    u = pb_ref[:, 0:D_BRANCH].astype(F32)
    vb = pb_ref[:, D_BRANCH:2 * D_BRANCH].astype(F32)
    mu = jnp.mean(vb, axis=-1, keepdims=True)
    xc = vb - mu
    var = jnp.mean(xc * xc, axis=-1, keepdims=True)
    vn = (xc * lax.rsqrt(var + EPS) * vg_ref[...]).astype(BF16)
    row = lax.broadcasted_iota(jnp.int32, (SG_CHUNK, SG_CHUNK), 0)
    col = lax.broadcasted_iota(jnp.int32, (SG_CHUNK, SG_CHUNK), 1)
    group_of_lane = lax.broadcasted_iota(jnp.int32, (1, D_BRANCH), 1) // HEAD_DIM
    ws = [jnp.where(row >= col, ws_ref[g], 0.0).astype(BF16) for g in range(N_HEADS)]
    mixed = []
    for ci in range(tm // SG_CHUNK):
        vchunk = vn[ci * SG_CHUNK:(ci + 1) * SG_CHUNK]
        acc = bs_ref[...]
        for g in range(N_HEADS):
            acc = acc + jnp.where(group_of_lane == g, _dot(ws[g], vchunk), 0.0)
        mixed.append(acc)
    yb = u * jnp.concatenate(mixed, axis=0)

    out = x_ref[...]
    out = out + _dot(gated(ya_ref[...].astype(F32), ga_ref, bga_ref), woa_ref[...])
    out = out + _dot(gated(yb, pb_ref.at[:, 2 * D_BRANCH:3 * D_BRANCH], bgb_ref), wob_ref[...])
    lane = lax.broadcasted_iota(jnp.int32, (1, HEAD_PAD), 1)

    def unpad(y_ref):
        halves = []
        for pr in range(N_HEADS // 2):
            even = y_ref[:, (2 * pr) * HEAD_PAD:(2 * pr + 1) * HEAD_PAD].astype(F32)
            odd = y_ref[:, (2 * pr + 1) * HEAD_PAD:(2 * pr + 2) * HEAD_PAD].astype(F32)
            halves.append(jnp.where(lane < HEAD_DIM, even, pltpu.roll(odd, HEAD_DIM, 1)))
        return jnp.concatenate(halves, axis=1)

    out = out + _dot(gated(unpad(yc_ref), gc_ref, bgc_ref), woc_ref[...])
    out = out + _dot(gated(unpad(yd_ref), gd_ref, bgd_ref), wod_ref[...])
    if last:
        ms = jnp.mean(out * out, axis=-1, keepdims=True)
        out = out * lax.rsqrt(ms + EPS) * fg_ref[...]
    o_ref[...] = out


def _out_proj(x, ya, pa, pb, yc, gc, yd, gd, bga, bgb, bgc, bgd, vg, ws, bs,
              woa, wob, woc, wod, fg, last):
    b, s, d = x.shape
    tm = TM_OUT
    tok = lambda n, colblk=0: pl.BlockSpec((None, tm, n), lambda bi, i: (bi, i, colblk))
    const = lambda a: pl.BlockSpec(a.shape, lambda bi, i: (0,) * a.ndim)
    return pl.pallas_call(
        functools.partial(_out_kernel, last=last),
        grid=(b, s // tm),
        in_specs=[tok(d), tok(D_BRANCH), tok(D_BRANCH, 3), tok(3 * D_BRANCH), tok(D_PAD), tok(D_BRANCH),
                  tok(D_PAD), tok(D_BRANCH), const(bga), const(bgb), const(bgc), const(bgd), const(vg),
                  const(ws), const(bs), const(woa), const(wob), const(woc), const(wod), const(fg)],
        out_specs=tok(d),
        out_shape=jax.ShapeDtypeStruct((b, s, d), F32),
        compiler_params=pltpu.CompilerParams(
            dimension_semantics=("arbitrary", "arbitrary"), vmem_limit_bytes=VMEM_LIMIT),
        name="out_proj",
    )(x, ya, pa, pb, yc, gc, yd, gd, bga, bgb, bgc, bgd, vg, ws, bs, woa, wob, woc, wod, fg)


def _layer(x, norm_g, w_in, b_f, rel_bias, w_s, b_s, v_gain, branch_gain, w_out, final_g, last):
    db = D_BRANCH
    o = 0
    wa = w_in[:, o:o + 4 * db]; o += 4 * db
    wb = w_in[:, o:o + 3 * db]; o += 3 * db
    wc = w_in[:, o:o + 4 * db]; o += 4 * db
    wf = w_in[:, o:o + N_HEADS]; o += N_HEADS
    wd = w_in[:, o:o + 4 * db]

    def q_scaled(w):
        return jnp.concatenate([w[:, :db] * SCALE, w[:, db:]], axis=1).astype(BF16)

    wf_p = jnp.pad(wf, ((0, 0), (0, HEAD_PAD - N_HEADS))).astype(BF16)
    bf_p = jnp.pad(b_f, (0, HEAD_PAD - N_HEADS)).reshape(1, HEAD_PAD)

    pa, pb, qc, kc, vc, gc, c2, qd, kd, vd, gd = _inproj(
        x, norm_g.reshape(1, -1), q_scaled(wa), wb.astype(BF16), q_scaled(wc), wf_p, q_scaled(wd), bf_p)

    ya = _chunk_attn(pa, _chunk_bias_row(rel_bias))
    c_row = jnp.transpose(c2[:, :, :N_HEADS], (0, 2, 1))[:, :, None, :]
    yc = _fox(qc, kc, jnp.swapaxes(vc, 2, 3), c_row)
    yd = _stick_breaking(qd, kd, vd)

    bs_tile = jnp.repeat(jnp.transpose(b_s), HEAD_DIM, axis=1)
    return _out_proj(
        x, ya, pa, pb, yc, gc, yd, gd,
        branch_gain[0].reshape(1, -1), branch_gain[1].reshape(1, -1),
        branch_gain[2].reshape(1, -1), branch_gain[3].reshape(1, -1),
        v_gain.reshape(1, -1), w_s, bs_tile,
        w_out[0:db].astype(BF16), w_out[db:2 * db].astype(BF16),
        w_out[2 * db:3 * db].astype(BF16), w_out[3 * db:4 * db].astype(BF16),
        final_g.reshape(1, -1), last)


def kernel(x, norm_g, w_in, b_f, rel_bias, w_s, b_s, v_gain, branch_gain, w_out, final_g):
    depth = norm_g.shape[0]
    for l in range(depth):
        x = _layer(x, norm_g[l], w_in[l], b_f[l], rel_bias[l], w_s[l], b_s[l], v_gain[l],
                   branch_gain[l], w_out[l], final_g, last=(l == depth - 1))
    return x
```

```python
import functools

import jax
import jax.numpy as jnp
import numpy as np
from jax import lax
from jax.experimental import pallas as pl
from jax.experimental.pallas import tpu as pltpu

D_MODEL = 1024
N_HEADS = 4
HEAD_DIM = 64
D_BRANCH = N_HEADS * HEAD_DIM
CHUNK = 64
LOOKBACK_CHUNKS = 8
MAX_REL = 128
SG_CHUNK = 128
EPS = 1e-6
SCALE = HEAD_DIM ** -0.5
HEAD_PAD = 128
D_PAD = N_HEADS * HEAD_PAD
NEG = -1e30
LOG2E = 1.4426950408889634
SB_EXIT = 151.0
FOX_EXIT = 152.0
NORM_SLACK = 1.01
NORM_ROWS = 2048

F32 = jnp.float32
BF16 = jnp.bfloat16

TM_IN = 512
TQ_A = 256
LOOK_A = LOOKBACK_CHUNKS * CHUNK
NBLK_A = LOOK_A // TQ_A + 1
WIN_A = LOOK_A + TQ_A
ROLL_A = 1024
TQ_C = 1024
TK_C = 512
KPQ_C = TQ_C // TK_C
TQ_D = 256
TM_OUT = 512
VMEM_LIMIT = 56 * 1024 * 1024


def _nt_dot(a, b):
    return lax.dot_general(a, b, (((1,), (1,)), ((), ())), preferred_element_type=F32)


def _dot(a, b):
    return jnp.dot(a, b, preferred_element_type=F32)


def _bf16_part(x):
    return x.astype(BF16).astype(F32)


def _split3(x):
    hi = _bf16_part(x)
    mid = _bf16_part(x - hi)
    return hi, mid, x - hi - mid


def _inproj_kernel(x_ref, g_ref, wa_ref, wb_ref, wc_ref, wf_ref, wd_ref, bf_ref,
                   pa_ref, pb_ref, qc_ref, kc_ref, vc_ref, gc_ref, c_ref,
                   qd_ref, kd_ref, vd_ref, gd_ref, carry_ref):
    i = pl.program_id(1)
    tm = x_ref.shape[0]
    x = x_ref[...]
    ms = jnp.mean(x * x, axis=-1, keepdims=True)
    h = (x * lax.rsqrt(ms + EPS) * g_ref[...]).astype(BF16)

    pa = _dot(h, wa_ref[...])
    pa_ref[:, :D_BRANCH] = (pa[:, :D_BRANCH] * LOG2E).astype(BF16)
    pa_ref[:, D_BRANCH:] = pa[:, D_BRANCH:].astype(BF16)
    pb_ref[...] = _dot(h, wb_ref[...]).astype(BF16)

    f = _dot(h, wf_ref[...]) + bf_ref[...]
    ls = jnp.minimum(f, 0.0) - jnp.log(1.0 + jnp.exp(-jnp.abs(f)))
    row = lax.broadcasted_iota(jnp.int32, (tm, tm), 0)
    col = lax.broadcasted_iota(jnp.int32, (tm, tm), 1)
    tri = (row >= col).astype(BF16)
    ls_hi, ls_mid, ls_lo = _split3(ls)
    parts = _dot(tri, jnp.concatenate([ls_hi, ls_mid, ls_lo], axis=1).astype(BF16))
    local = parts[:, :HEAD_PAD] + parts[:, HEAD_PAD:2 * HEAD_PAD] + parts[:, 2 * HEAD_PAD:]

    @pl.when(i == 0)
    def _():
        carry_ref[...] = jnp.zeros_like(carry_ref)

    c = local + carry_ref[...]
    carry_ref[...] = c[tm - 1:tm, :]
    c2 = c * LOG2E
    c_ref[...] = c2.T

    lane = lax.broadcasted_iota(jnp.int32, (1, HEAD_PAD), 1)
    a0 = HEAD_DIM
    k_const = ((lane >= a0) & (lane < a0 + 3)).astype(F32)
    q_const = -((lane >= a0 + 3) & (lane < a0 + 6)).astype(F32)
    v_const = (lane == a0).astype(F32)

    def head_tile(p, part, hh, spare, scale=None, transpose=False):
        lo = part * D_BRANCH + (hh // 2) * HEAD_PAD
        tile = p[:, lo:lo + HEAD_PAD]
        if hh % 2:
            tile = pltpu.roll(tile, HEAD_DIM, 1)
        if scale is not None:
            tile = tile * scale
        tile = jnp.where(lane < HEAD_DIM, tile, spare)
        return (tile.T if transpose else tile).astype(BF16)

    pc = _dot(h, wc_ref[...])
    pieces = _split3(c2)
    for hh in range(N_HEADS):
        hi, mid, lo = (jnp.broadcast_to(p[:, hh:hh + 1], (tm, HEAD_PAD)) for p in pieces)
        q_spare = jnp.where(lane == a0, hi, jnp.where(lane == a0 + 1, mid, jnp.where(lane == a0 + 2, lo, q_const)))
        k_spare = jnp.where(lane == a0 + 3, hi, jnp.where(lane == a0 + 4, mid, jnp.where(lane == a0 + 5, lo, k_const)))
        qc_ref[hh] = head_tile(pc, 0, hh, q_spare, LOG2E)
        kc_ref[hh] = head_tile(pc, 1, hh, k_spare)
        vc_ref[hh] = head_tile(pc, 2, hh, v_const, transpose=True)
    gc_ref[...] = pc[:, 3 * D_BRANCH:4 * D_BRANCH].astype(BF16)

    pd = _dot(h, wd_ref[...])
    for hh in range(N_HEADS):
        qd_ref[hh] = head_tile(pd, 0, hh, 0.0, LOG2E)
        kd_ref[hh] = head_tile(pd, 1, hh, 0.0)
        vd_ref[hh] = head_tile(pd, 2, hh, 0.0)
    gd_ref[...] = pd[:, 3 * D_BRANCH:4 * D_BRANCH].astype(BF16)


def _inproj(x, norm_g, wa, wb, wc, wf, wd, bf):
    b, s, d = x.shape
    tm = TM_IN
    const = lambda shape: pl.BlockSpec(shape, lambda bi, i: (0,) * len(shape))
    tok = lambda n: pl.BlockSpec((None, tm, n), lambda bi, i: (bi, i, 0))
    head = pl.BlockSpec((None, N_HEADS, tm, HEAD_PAD), lambda bi, i: (bi, 0, i, 0))
    head_t = pl.BlockSpec((None, N_HEADS, HEAD_PAD, tm), lambda bi, i: (bi, 0, 0, i))
    head_shape = jax.ShapeDtypeStruct((b, N_HEADS, s, HEAD_PAD), BF16)
    return pl.pallas_call(
        _inproj_kernel,
        grid=(b, s // tm),
        in_specs=[tok(d), const((1, d)), const(wa.shape), const(wb.shape), const(wc.shape),
                  const(wf.shape), const(wd.shape), const((1, HEAD_PAD))],
        out_specs=[tok(wa.shape[1]), tok(wb.shape[1]), head, head, head_t, tok(D_BRANCH),
                   pl.BlockSpec((None, HEAD_PAD, tm), lambda bi, i: (bi, 0, i)),
                   head, head, head, tok(D_BRANCH)],
        out_shape=[jax.ShapeDtypeStruct((b, s, wa.shape[1]), BF16),
                   jax.ShapeDtypeStruct((b, s, wb.shape[1]), BF16),
                   head_shape, head_shape,
                   jax.ShapeDtypeStruct((b, N_HEADS, HEAD_PAD, s), BF16),
                   jax.ShapeDtypeStruct((b, s, D_BRANCH), BF16),
                   jax.ShapeDtypeStruct((b, HEAD_PAD, s), F32),
                   head_shape, head_shape, head_shape,
                   jax.ShapeDtypeStruct((b, s, D_BRANCH), BF16)],
        scratch_shapes=[pltpu.VMEM((1, HEAD_PAD), F32)],
        compiler_params=pltpu.CompilerParams(
            dimension_semantics=("arbitrary", "arbitrary"), vmem_limit_bytes=VMEM_LIMIT),
        name="inproj",
    )(x, norm_g, wa, wb, wc, wf, wd, bf)


def _chunk_attn_kernel(q_ref, *refs):
    k_refs, v_refs = refs[:NBLK_A], refs[NBLK_A:2 * NBLK_A]
    rb_ref, o_ref, bias_ref = refs[2 * NBLK_A:]
    i = pl.program_id(1)
    tq = q_ref.shape[0]

    @pl.when((pl.program_id(0) == 0) & (i == 0))
    def _():
        ti = lax.broadcasted_iota(jnp.int32, (tq, WIN_A), 0) // CHUNK
        sj = lax.broadcasted_iota(jnp.int32, (tq, WIN_A), 1) // CHUNK
        band = (sj >= ti) & (sj <= ti + LOOKBACK_CHUNKS)
        for hh in range(N_HEADS):
            rows = jnp.broadcast_to(rb_ref[hh:hh + 1, :], (tq, ROLL_A))
            toep = pltpu.roll(rows, 0, 1, stride=1, stride_axis=0)
            bias_ref[hh] = jnp.where(band, toep[:, :WIN_A], NEG)

    q = q_ref[...]
    k = jnp.concatenate([r[...] for r in k_refs], axis=0)
    v = jnp.concatenate([r[...] for r in v_refs], axis=0)
    head_of_lane = lax.broadcasted_iota(jnp.int32, (1, D_BRANCH), 1) // HEAD_DIM
    col = lax.broadcasted_iota(jnp.int32, (tq, WIN_A), 1)
    in_seq = col >= LOOK_A - i * tq
    out = jnp.zeros((tq, D_BRANCH), F32)
    for hh in range(N_HEADS):
        sel = head_of_lane == hh
        qh = jnp.where(sel, q, jnp.zeros_like(q))
        s = _nt_dot(qh, k) + bias_ref[hh]
        s = jnp.where(in_seq, s, NEG)
        m = jnp.max(s, axis=-1, keepdims=True)
        p = jnp.exp2(s - m)
        l = jnp.sum(p, axis=-1, keepdims=True)
        pv = _dot(p.astype(BF16), v)
        out = out + jnp.where(sel, pv * (1.0 / l), 0.0)
    o_ref[...] = out.astype(o_ref.dtype)


def _chunk_attn(pa, rb_row):
    b, s, _ = pa.shape
    tq = TQ_A
    blk = lambda colblk, back: pl.BlockSpec(
        (None, tq, D_BRANCH), lambda bi, i: (bi, jnp.maximum(i - back, 0), colblk))
    return pl.pallas_call(
        _chunk_attn_kernel,
        grid=(b, s // tq),
        in_specs=([blk(0, 0)] + [blk(1, back) for back in reversed(range(NBLK_A))]
                  + [blk(2, back) for back in reversed(range(NBLK_A))]
                  + [pl.BlockSpec(rb_row.shape, lambda bi, i: (0, 0))]),
        out_specs=pl.BlockSpec((None, tq, D_BRANCH), lambda bi, i: (bi, i, 0)),
        out_shape=jax.ShapeDtypeStruct((b, s, D_BRANCH), BF16),
        scratch_shapes=[pltpu.VMEM((N_HEADS, tq, WIN_A), F32)],
        compiler_params=pltpu.CompilerParams(
            dimension_semantics=("arbitrary", "arbitrary"), vmem_limit_bytes=VMEM_LIMIT),
        name="chunk_attn",
    )(*([pa] * (1 + 2 * NBLK_A)), rb_row)


def _chunk_bias_row(rel_bias):
    u = np.arange(ROLL_A)
    e = np.where(u < WIN_A, u, u - ROLL_A)
    rel = np.clip(LOOK_A - e, -MAX_REL, MAX_REL) + MAX_REL
    return rel_bias[:, rel].astype(F32) * LOG2E


def _fox_kernel(cend_ref, q_ref, k_ref, vt_ref, cq_ref, o_ref,
                sa_ref, sb_ref, ma_ref, mb_ref, m_ref, acc_ref, kmax_ref):
    bi, hi, i = pl.program_id(0), pl.program_id(1), pl.program_id(2)
    tq = q_ref.shape[0]
    tk = sa_ref.shape[0]
    s_len = k_ref.shape[0]
    q = q_ref[...]
    m_ref[...] = jnp.full_like(m_ref, NEG)
    acc_ref[...] = jnp.zeros_like(acc_ref)
    lane = lax.broadcasted_iota(jnp.int32, (1, HEAD_PAD), 1)

    @pl.when(i == 0)
    def _():
        def blk(r, mx):
            kf = k_ref[pl.ds(pl.multiple_of(r * NORM_ROWS, NORM_ROWS), NORM_ROWS), :].astype(F32)
            k2 = jnp.sum(jnp.where(lane < HEAD_DIM, kf * kf, 0.0), axis=1, keepdims=True)
            return jnp.maximum(mx, jnp.max(k2))
        kmax_ref[0] = jnp.sqrt(lax.fori_loop(0, s_len // NORM_ROWS, blk, jnp.float32(0.0)))

    qf = q.astype(F32)
    q2 = jnp.where(lane < HEAD_DIM, qf * qf, 0.0).astype(BF16)
    n2 = _nt_dot(jnp.ones((8, HEAD_PAD), BF16), q2)[0:1, :]
    reach = jnp.sqrt(n2) * (kmax_ref[0] * NORM_SLACK) + cq_ref[...]
    cbase = (bi * pl.num_programs(1) + hi) * (s_len // tk)

    def scores(j, s_ref, cm_ref, diag_off=None):
        ks = pl.multiple_of(jnp.maximum(j, 0) * tk, tk)
        st = _nt_dot(k_ref[pl.ds(ks, tk), :], q)
        if diag_off is not None:
            kp = lax.broadcasted_iota(jnp.int32, (tk, tq), 0) + diag_off
            qp = lax.broadcasted_iota(jnp.int32, (tk, tq), 1)
            st = jnp.where(kp <= qp, st, NEG)
        s_ref[...] = st
        cm_ref[...] = jnp.max(st, axis=0, keepdims=True)

    def consume(j, s_ref, cm_ref):
        ks = pl.multiple_of(j * tk, tk)
        vt = vt_ref[:, pl.ds(ks, tk)]
        m_prev = m_ref[...]
        m_new = jnp.maximum(m_prev, cm_ref[...])
        alpha = jnp.exp2(m_prev - m_new)
        pt = jnp.exp2(s_ref[...] - m_new).astype(BF16)
        acc_ref[...] = alpha * acc_ref[...] + _dot(vt, pt)
        m_ref[...] = m_new

    top = KPQ_C * i + 1
    scores(top, sa_ref, ma_ref, tk)
    scores(top - 1, sb_ref, mb_ref, 0)
    consume(top, sa_ref, ma_ref)
    scores(top - 2, sa_ref, ma_ref)
    consume(top - 1, sb_ref, mb_ref)

    def cond(state):
        p, g = state
        return (p <= i) & (g - cend_ref[cbase + jnp.maximum(top - 2 * p, 0)] >= -FOX_EXIT)

    def pair(state):
        p, _ = state
        g = jnp.max(reach - m_ref[...])
        j = top - 2 * p
        scores(j - 1, sb_ref, mb_ref)
        consume(j, sa_ref, ma_ref)
        scores(j - 2, sa_ref, ma_ref)
        consume(j - 1, sb_ref, mb_ref)
        return p + 1, g

    lax.while_loop(cond, pair, (jnp.int32(1), jnp.max(reach - m_ref[...])))

    acc = acc_ref[...]
    out_t = acc * (1.0 / acc[HEAD_DIM:HEAD_DIM + 1, :])
    d_idx = lax.broadcasted_iota(jnp.int32, out_t.shape, 0)
    o_ref[...] = jnp.where(d_idx < HEAD_DIM, out_t, 0.0).T.astype(o_ref.dtype)


def _fox(q, k, vt, c_row):
    b, nh, s, _ = q.shape
    tq, tk = TQ_C, TK_C
    cend = c_row[:, :, 0, tk - 1::tk].reshape(-1)
    qblk = pl.BlockSpec((None, None, tq, HEAD_PAD), lambda bi, hi, i, ce: (bi, hi, i, 0))
    full = pl.BlockSpec((None, None, s, HEAD_PAD), lambda bi, hi, i, ce: (bi, hi, 0, 0))
    full_t = pl.BlockSpec((None, None, HEAD_PAD, s), lambda bi, hi, i, ce: (bi, hi, 0, 0))
    crow = pl.BlockSpec((None, None, 1, tq), lambda bi, hi, i, ce: (bi, hi, 0, i))
    return pl.pallas_call(
        _fox_kernel,
        grid_spec=pltpu.PrefetchScalarGridSpec(
            num_scalar_prefetch=1,
            grid=(b, nh, s // tq),
            in_specs=[qblk, full, full_t, crow],
            out_specs=pl.BlockSpec((None, tq, HEAD_PAD), lambda bi, hi, i, ce: (bi, i, hi)),
            scratch_shapes=[pltpu.VMEM((tk, tq), F32), pltpu.VMEM((tk, tq), F32),
                            pltpu.VMEM((1, tq), F32), pltpu.VMEM((1, tq), F32),
                            pltpu.VMEM((1, tq), F32), pltpu.VMEM((HEAD_PAD, tq), F32),
                            pltpu.SMEM((1,), F32)]),
        out_shape=jax.ShapeDtypeStruct((b, s, D_PAD), BF16),
        compiler_params=pltpu.CompilerParams(
            dimension_semantics=("arbitrary", "arbitrary", "arbitrary"),
            vmem_limit_bytes=VMEM_LIMIT),
        name="fox",
    )(cend, q, k, vt, c_row)


def _sb_kernel(q_ref, k_ref, v_ref, o_ref, r_ref, acc_ref):
    i = pl.program_id(1)
    nh, t = q_ref.shape[0], q_ref.shape[1]
    row = lax.broadcasted_iota(jnp.int32, (t, t), 0)
    col = lax.broadcasted_iota(jnp.int32, (t, t), 1)
    tri = (row >= col).astype(BF16)
    tri2 = jnp.concatenate([tri, tri], axis=0)

    def tile(hh, kb, keep):
        ks = pl.multiple_of(kb * t, t)
        k = k_ref[hh, pl.ds(ks, t), :]
        v = v_ref[hh, pl.ds(ks, t), :]
        z = _nt_dot(q_ref[hh], k)
        sp = jnp.maximum(z, 0.0) + jnp.log2(1.0 + jnp.exp2(-jnp.abs(z)))
        if keep is not None:
            sp = jnp.where(keep, sp, 0.0)
        hi = sp.astype(BF16)
        lo = (sp - hi.astype(F32)).astype(BF16)
        suf = _dot(jnp.concatenate([hi, lo], axis=1), tri2)
        return z, suf, v

    keep_a = col < row
    rmin = None
    for hh in range(nh):
        z_a, suf_a, v_a = tile(hh, i, keep_a)
        z_b, suf_b, v_b = tile(hh, jnp.maximum(i - 1, 0), None)
        v_b = jnp.where(i > 0, v_b, jnp.zeros_like(v_b))
        tot_a = suf_a[:, 0:1]
        a_a = jnp.where(keep_a, jnp.exp2(z_a - suf_a), 0.0)
        a_b = jnp.exp2(z_b - suf_b - tot_a)
        acc_ref[hh] = _dot(a_a.astype(BF16), v_a) + _dot(a_b.astype(BF16), v_b)
        r0 = tot_a + suf_b[:, 0:1]
        r_ref[hh] = r0
        rmin = jnp.min(r0) if rmin is None else jnp.minimum(rmin, jnp.min(r0))

    def cond(state):
        kb, rm = state
        return (kb >= 0) & (rm < SB_EXIT)

    def body(state):
        kb, _ = state
        rm = None
        for hh in range(nh):
            z, suf, v = tile(hh, kb, None)
            r = r_ref[hh]
            a = jnp.exp2(z - suf - r)
            acc_ref[hh] += _dot(a.astype(BF16), v)
            r = r + suf[:, 0:1]
            r_ref[hh] = r
            rm = jnp.min(r) if rm is None else jnp.minimum(rm, jnp.min(r))
        return kb - 1, rm

    lax.while_loop(cond, body, (i - 2, rmin))
    for hh in range(nh):
        o_ref[:, hh * HEAD_PAD:(hh + 1) * HEAD_PAD] = acc_ref[hh].astype(o_ref.dtype)


def _stick_breaking(q, k, v):
    b, nh, s, _ = q.shape
    tq = TQ_D
    qblk = pl.BlockSpec((None, nh, tq, HEAD_PAD), lambda bi, i: (bi, 0, i, 0))
    full = pl.BlockSpec((None, nh, s, HEAD_PAD), lambda bi, i: (bi, 0, 0, 0),
                        pipeline_mode=pl.Buffered(1))
    return pl.pallas_call(
        _sb_kernel,
        grid=(b, s // tq),
        in_specs=[qblk, full, full],
        out_specs=pl.BlockSpec((None, tq, D_PAD), lambda bi, i: (bi, i, 0)),
        out_shape=jax.ShapeDtypeStruct((b, s, D_PAD), BF16),
        scratch_shapes=[pltpu.VMEM((nh, tq, 1), F32), pltpu.VMEM((nh, tq, HEAD_PAD), F32)],
        compiler_params=pltpu.CompilerParams(
            dimension_semantics=("arbitrary", "arbitrary"),
            vmem_limit_bytes=VMEM_LIMIT),
        name="stick_breaking",
    )(q, k, v)


def _out_kernel(x_ref, ya_ref, ga_ref, pb_ref, yc_ref, gc_ref, yd_ref, gd_ref,
                bga_ref, bgb_ref, bgc_ref, bgd_ref, vg_ref, ws_ref, bs_ref,
                woa_ref, wob_ref, woc_ref, wod_ref, fg_ref, o_ref, *, last):
    tm = x_ref.shape[0]

    def gated(y, g_ref, gain_ref):
        ms = jnp.sum(y * y, axis=-1, keepdims=True) * (1.0 / D_BRANCH)
        g = g_ref[...].astype(F32)
        silu = g * (1.0 / (1.0 + jnp.exp(-g)))
        return (y * lax.rsqrt(ms + EPS) * gain_ref[...] * silu).astype(BF16)

    u = pb_ref[:, 0:D_BRANCH].astype(F32)
    vb = pb_ref[:, D_BRANCH:2 * D_BRANCH].astype(F32)
    mu = jnp.mean(vb, axis=-1, keepdims=True)
    xc = vb - mu
    var = jnp.mean(xc * xc, axis=-1, keepdims=True)
    vn = (xc * lax.rsqrt(var + EPS) * vg_ref[...]).astype(BF16)
    row = lax.broadcasted_iota(jnp.int32, (SG_CHUNK, SG_CHUNK), 0)
    col = lax.broadcasted_iota(jnp.int32, (SG_CHUNK, SG_CHUNK), 1)
    group_of_lane = lax.broadcasted_iota(jnp.int32, (1, D_BRANCH), 1) // HEAD_DIM
    ws = [jnp.where(row >= col, ws_ref[g], 0.0).astype(BF16) for g in range(N_HEADS)]
    mixed = []
    for ci in range(tm // SG_CHUNK):
        vchunk = vn[ci * SG_CHUNK:(ci + 1) * SG_CHUNK]
        acc = bs_ref[...]
        for g in range(N_HEADS):
            acc = acc + jnp.where(group_of_lane == g, _dot(ws[g], vchunk), 0.0)
        mixed.append(acc)
    yb = u * jnp.concatenate(mixed, axis=0)

    out = x_ref[...]
    out = out + _dot(gated(ya_ref[...].astype(F32), ga_ref, bga_ref), woa_ref[...])
    out = out + _dot(gated(yb, pb_ref.at[:, 2 * D_BRANCH:3 * D_BRANCH], bgb_ref), wob_ref[...])
    lane = lax.broadcasted_iota(jnp.int32, (1, HEAD_PAD), 1)

    def unpad(y_ref):
        halves = []
        for pr in range(N_HEADS // 2):
            even = y_ref[:, (2 * pr) * HEAD_PAD:(2 * pr + 1) * HEAD_PAD].astype(F32)
            odd = y_ref[:, (2 * pr + 1) * HEAD_PAD:(2 * pr + 2) * HEAD_PAD].astype(F32)
            halves.append(jnp.where(lane < HEAD_DIM, even, pltpu.roll(odd, HEAD_DIM, 1)))
        return jnp.concatenate(halves, axis=1)

    out = out + _dot(gated(unpad(yc_ref), gc_ref, bgc_ref), woc_ref[...])
    out = out + _dot(gated(unpad(yd_ref), gd_ref, bgd_ref), wod_ref[...])
    if last:
        ms = jnp.mean(out * out, axis=-1, keepdims=True)
        out = out * lax.rsqrt(ms + EPS) * fg_ref[...]
    o_ref[...] = out


def _out_proj(x, ya, pa, pb, yc, gc, yd, gd, bga, bgb, bgc, bgd, vg, ws, bs,
              woa, wob, woc, wod, fg, last):
    b, s, d = x.shape
    tm = TM_OUT
    tok = lambda n, colblk=0: pl.BlockSpec((None, tm, n), lambda bi, i: (bi, i, colblk))
    const = lambda a: pl.BlockSpec(a.shape, lambda bi, i: (0,) * a.ndim)
    return pl.pallas_call(
        functools.partial(_out_kernel, last=last),
        grid=(b, s // tm),
        in_specs=[tok(d), tok(D_BRANCH), tok(D_BRANCH, 3), tok(3 * D_BRANCH), tok(D_PAD), tok(D_BRANCH),
                  tok(D_PAD), tok(D_BRANCH), const(bga), const(bgb), const(bgc), const(bgd), const(vg),
                  const(ws), const(bs), const(woa), const(wob), const(woc), const(wod), const(fg)],
        out_specs=tok(d),
        out_shape=jax.ShapeDtypeStruct((b, s, d), F32),
        compiler_params=pltpu.CompilerParams(
            dimension_semantics=("arbitrary", "arbitrary"), vmem_limit_bytes=VMEM_LIMIT),
        name="out_proj",
    )(x, ya, pa, pb, yc, gc, yd, gd, bga, bgb, bgc, bgd, vg, ws, bs, woa, wob, woc, wod, fg)


def _layer(x, norm_g, w_in, b_f, rel_bias, w_s, b_s, v_gain, branch_gain, w_out, final_g, last):
    db = D_BRANCH
    o = 0
    wa = w_in[:, o:o + 4 * db]; o += 4 * db
    wb = w_in[:, o:o + 3 * db]; o += 3 * db
    wc = w_in[:, o:o + 4 * db]; o += 4 * db
    wf = w_in[:, o:o + N_HEADS]; o += N_HEADS
    wd = w_in[:, o:o + 4 * db]

    def q_scaled(w):
        return jnp.concatenate([w[:, :db] * SCALE, w[:, db:]], axis=1).astype(BF16)

    wf_p = jnp.pad(wf, ((0, 0), (0, HEAD_PAD - N_HEADS))).astype(BF16)
    bf_p = jnp.pad(b_f, (0, HEAD_PAD - N_HEADS)).reshape(1, HEAD_PAD)

    pa, pb, qc, kc, vc, gc, c2, qd, kd, vd, gd = _inproj(
        x, norm_g.reshape(1, -1), q_scaled(wa), wb.astype(BF16), q_scaled(wc), wf_p, q_scaled(wd), bf_p)

    ya = _chunk_attn(pa, _chunk_bias_row(rel_bias))
    yc = _fox(qc, kc, vc, c2[:, :N_HEADS, None, :])
    yd = _stick_breaking(qd, kd, vd)

    bs_tile = jnp.repeat(jnp.transpose(b_s), HEAD_DIM, axis=1)
    return _out_proj(
        x, ya, pa, pb, yc, gc, yd, gd,
        branch_gain[0].reshape(1, -1), branch_gain[1].reshape(1, -1),
        branch_gain[2].reshape(1, -1), branch_gain[3].reshape(1, -1),
        v_gain.reshape(1, -1), w_s, bs_tile,
        w_out[0:db].astype(BF16), w_out[db:2 * db].astype(BF16),
        w_out[2 * db:3 * db].astype(BF16), w_out[3 * db:4 * db].astype(BF16),
        final_g.reshape(1, -1), last)


def kernel(x, norm_g, w_in, b_f, rel_bias, w_s, b_s, v_gain, branch_gain, w_out, final_g):
    depth = norm_g.shape[0]
    for l in range(depth):
        x = _layer(x, norm_g[l], w_in[l], b_f[l], rel_bias[l], w_s[l], b_s[l], v_gain[l],
                   branch_gain[l], w_out[l], final_g, last=(l == depth - 1))
    return x
```

```python
import functools

import jax
import jax.numpy as jnp
import numpy as np
from jax import lax
from jax.experimental import pallas as pl
from jax.experimental.pallas import tpu as pltpu

D_MODEL = 1024
N_HEADS = 4
HEAD_DIM = 64
D_BRANCH = N_HEADS * HEAD_DIM
CHUNK = 64
LOOKBACK_CHUNKS = 8
MAX_REL = 128
SG_CHUNK = 128
EPS = 1e-6
SCALE = HEAD_DIM ** -0.5
HEAD_PAD = 128
D_PAD = N_HEADS * HEAD_PAD
NEG = -1e30
LOG2E = 1.4426950408889634
SB_EXIT = 151.0
FOX_EXIT = 152.0
NORM_SLACK = 1.01
NORM_ROWS = 2048

F32 = jnp.float32
BF16 = jnp.bfloat16

TM_IN = 512
TQ_A = 256
LOOK_A = LOOKBACK_CHUNKS * CHUNK
NBLK_A = LOOK_A // TQ_A + 1
WIN_A = LOOK_A + TQ_A
ROLL_A = 1024
TQ_C = 1024
TK_C = 512
KPQ_C = TQ_C // TK_C
TQ_D = 256
SUB_D = 2
TM_OUT = 512
VMEM_LIMIT = 56 * 1024 * 1024


def _nt_dot(a, b):
    return lax.dot_general(a, b, (((1,), (1,)), ((), ())), preferred_element_type=F32)


def _dot(a, b):
    return jnp.dot(a, b, preferred_element_type=F32)


def _bf16_part(x):
    return x.astype(BF16).astype(F32)


def _split3(x):
    hi = _bf16_part(x)
    mid = _bf16_part(x - hi)
    return hi, mid, x - hi - mid


def _inproj_kernel(x_ref, g_ref, wa_ref, wb_ref, wc_ref, wf_ref, wd_ref, bf_ref,
                   pa_ref, pb_ref, qc_ref, kc_ref, vc_ref, gc_ref, c_ref,
                   qd_ref, kd_ref, vd_ref, gd_ref, carry_ref):
    i = pl.program_id(1)
    tm = x_ref.shape[0]
    x = x_ref[...]
    ms = jnp.mean(x * x, axis=-1, keepdims=True)
    h = (x * lax.rsqrt(ms + EPS) * g_ref[...]).astype(BF16)

    pa = _dot(h, wa_ref[...])
    pa_ref[:, :D_BRANCH] = (pa[:, :D_BRANCH] * LOG2E).astype(BF16)
    pa_ref[:, D_BRANCH:] = pa[:, D_BRANCH:].astype(BF16)
    pb_ref[...] = _dot(h, wb_ref[...]).astype(BF16)

    f = _dot(h, wf_ref[...]) + bf_ref[...]
    ls = jnp.minimum(f, 0.0) - jnp.log(1.0 + jnp.exp(-jnp.abs(f)))
    row = lax.broadcasted_iota(jnp.int32, (tm, tm), 0)
    col = lax.broadcasted_iota(jnp.int32, (tm, tm), 1)
    tri = (row >= col).astype(BF16)
    ls_hi, ls_mid, ls_lo = _split3(ls)
    parts = _dot(tri, jnp.concatenate([ls_hi, ls_mid, ls_lo], axis=1).astype(BF16))
    local = parts[:, :HEAD_PAD] + parts[:, HEAD_PAD:2 * HEAD_PAD] + parts[:, 2 * HEAD_PAD:]

    @pl.when(i == 0)
    def _():
        carry_ref[...] = jnp.zeros_like(carry_ref)

    c = local + carry_ref[...]
    carry_ref[...] = c[tm - 1:tm, :]
    c2 = c * LOG2E
    c_ref[...] = c2.T

    lane = lax.broadcasted_iota(jnp.int32, (1, HEAD_PAD), 1)
    a0 = HEAD_DIM
    k_const = ((lane >= a0) & (lane < a0 + 3)).astype(F32)
    q_const = -((lane >= a0 + 3) & (lane < a0 + 6)).astype(F32)
    v_const = (lane == a0).astype(F32)

    def head_tile(p, part, hh, spare, scale=None, transpose=False):
        lo = part * D_BRANCH + (hh // 2) * HEAD_PAD
        tile = p[:, lo:lo + HEAD_PAD]
        if hh % 2:
            tile = pltpu.roll(tile, HEAD_DIM, 1)
        if scale is not None:
            tile = tile * scale
        tile = jnp.where(lane < HEAD_DIM, tile, spare)
        return (tile.T if transpose else tile).astype(BF16)

    pc = _dot(h, wc_ref[...])
    pieces = _split3(c2)
    for hh in range(N_HEADS):
        hi, mid, lo = (jnp.broadcast_to(p[:, hh:hh + 1], (tm, HEAD_PAD)) for p in pieces)
        q_spare = jnp.where(lane == a0, hi, jnp.where(lane == a0 + 1, mid, jnp.where(lane == a0 + 2, lo, q_const)))
        k_spare = jnp.where(lane == a0 + 3, hi, jnp.where(lane == a0 + 4, mid, jnp.where(lane == a0 + 5, lo, k_const)))
        qc_ref[hh] = head_tile(pc, 0, hh, q_spare, LOG2E)
        kc_ref[hh] = head_tile(pc, 1, hh, k_spare)
        vc_ref[hh] = head_tile(pc, 2, hh, v_const, transpose=True)
    gc_ref[...] = pc[:, 3 * D_BRANCH:4 * D_BRANCH].astype(BF16)

    pd = _dot(h, wd_ref[...])
    for hh in range(N_HEADS):
        qd_ref[hh] = head_tile(pd, 0, hh, 0.0, LOG2E)
        kd_ref[hh] = head_tile(pd, 1, hh, 0.0)
        vd_ref[hh] = head_tile(pd, 2, hh, 0.0)
    gd_ref[...] = pd[:, 3 * D_BRANCH:4 * D_BRANCH].astype(BF16)


def _inproj(x, norm_g, wa, wb, wc, wf, wd, bf):
    b, s, d = x.shape
    tm = TM_IN
    const = lambda shape: pl.BlockSpec(shape, lambda bi, i: (0,) * len(shape))
    tok = lambda n: pl.BlockSpec((None, tm, n), lambda bi, i: (bi, i, 0))
    head = pl.BlockSpec((None, N_HEADS, tm, HEAD_PAD), lambda bi, i: (bi, 0, i, 0))
    head_t = pl.BlockSpec((None, N_HEADS, HEAD_PAD, tm), lambda bi, i: (bi, 0, 0, i))
    head_shape = jax.ShapeDtypeStruct((b, N_HEADS, s, HEAD_PAD), BF16)
    return pl.pallas_call(
        _inproj_kernel,
        grid=(b, s // tm),
        in_specs=[tok(d), const((1, d)), const(wa.shape), const(wb.shape), const(wc.shape),
                  const(wf.shape), const(wd.shape), const((1, HEAD_PAD))],
        out_specs=[tok(wa.shape[1]), tok(wb.shape[1]), head, head, head_t, tok(D_BRANCH),
                   pl.BlockSpec((None, HEAD_PAD, tm), lambda bi, i: (bi, 0, i)),
                   head, head, head, tok(D_BRANCH)],
        out_shape=[jax.ShapeDtypeStruct((b, s, wa.shape[1]), BF16),
                   jax.ShapeDtypeStruct((b, s, wb.shape[1]), BF16),
                   head_shape, head_shape,
                   jax.ShapeDtypeStruct((b, N_HEADS, HEAD_PAD, s), BF16),
                   jax.ShapeDtypeStruct((b, s, D_BRANCH), BF16),
                   jax.ShapeDtypeStruct((b, HEAD_PAD, s), F32),
                   head_shape, head_shape, head_shape,
                   jax.ShapeDtypeStruct((b, s, D_BRANCH), BF16)],
        scratch_shapes=[pltpu.VMEM((1, HEAD_PAD), F32)],
        compiler_params=pltpu.CompilerParams(
            dimension_semantics=("arbitrary", "arbitrary"), vmem_limit_bytes=VMEM_LIMIT),
        name="inproj",
    )(x, norm_g, wa, wb, wc, wf, wd, bf)


def _chunk_attn_kernel(q_ref, *refs):
    k_refs, v_refs = refs[:NBLK_A], refs[NBLK_A:2 * NBLK_A]
    rb_ref, o_ref, bias_ref = refs[2 * NBLK_A:]
    i = pl.program_id(1)
    tq = q_ref.shape[0]

    @pl.when((pl.program_id(0) == 0) & (i == 0))
    def _():
        ti = lax.broadcasted_iota(jnp.int32, (tq, WIN_A), 0) // CHUNK
        sj = lax.broadcasted_iota(jnp.int32, (tq, WIN_A), 1) // CHUNK
        band = (sj >= ti) & (sj <= ti + LOOKBACK_CHUNKS)
        for hh in range(N_HEADS):
            rows = jnp.broadcast_to(rb_ref[hh:hh + 1, :], (tq, ROLL_A))
            toep = pltpu.roll(rows, 0, 1, stride=1, stride_axis=0)
            bias_ref[hh] = jnp.where(band, toep[:, :WIN_A], NEG)

    q = q_ref[...]
    k = jnp.concatenate([r[...] for r in k_refs], axis=0)
    v = jnp.concatenate([r[...] for r in v_refs], axis=0)
    head_of_lane = lax.broadcasted_iota(jnp.int32, (1, D_BRANCH), 1) // HEAD_DIM
    col = lax.broadcasted_iota(jnp.int32, (tq, WIN_A), 1)
    in_seq = col >= LOOK_A - i * tq
    out = jnp.zeros((tq, D_BRANCH), F32)
    for hh in range(N_HEADS):
        sel = head_of_lane == hh
        qh = jnp.where(sel, q, jnp.zeros_like(q))
        s = _nt_dot(qh, k) + bias_ref[hh]
        s = jnp.where(in_seq, s, NEG)
        m = jnp.max(s, axis=-1, keepdims=True)
        p = jnp.exp2(s - m)
        l = jnp.sum(p, axis=-1, keepdims=True)
        pv = _dot(p.astype(BF16), v)
        out = out + jnp.where(sel, pv * (1.0 / l), 0.0)
    o_ref[...] = out.astype(o_ref.dtype)


def _chunk_attn(pa, rb_row):
    b, s, _ = pa.shape
    tq = TQ_A
    blk = lambda colblk, back: pl.BlockSpec(
        (None, tq, D_BRANCH), lambda bi, i: (bi, jnp.maximum(i - back, 0), colblk))
    return pl.pallas_call(
        _chunk_attn_kernel,
        grid=(b, s // tq),
        in_specs=([blk(0, 0)] + [blk(1, back) for back in reversed(range(NBLK_A))]
                  + [blk(2, back) for back in reversed(range(NBLK_A))]
                  + [pl.BlockSpec(rb_row.shape, lambda bi, i: (0, 0))]),
        out_specs=pl.BlockSpec((None, tq, D_BRANCH), lambda bi, i: (bi, i, 0)),
        out_shape=jax.ShapeDtypeStruct((b, s, D_BRANCH), BF16),
        scratch_shapes=[pltpu.VMEM((N_HEADS, tq, WIN_A), F32)],
        compiler_params=pltpu.CompilerParams(
            dimension_semantics=("arbitrary", "arbitrary"), vmem_limit_bytes=VMEM_LIMIT),
        name="chunk_attn",
    )(*([pa] * (1 + 2 * NBLK_A)), rb_row)


def _chunk_bias_row(rel_bias):
    u = np.arange(ROLL_A)
    e = np.where(u < WIN_A, u, u - ROLL_A)
    rel = np.clip(LOOK_A - e, -MAX_REL, MAX_REL) + MAX_REL
    return rel_bias[:, rel].astype(F32) * LOG2E


def _fox_kernel(cend_ref, q_ref, k_ref, vt_ref, cq_ref, o_ref,
                sa_ref, sb_ref, ma_ref, mb_ref, m_ref, acc_ref, kmax_ref):
    bi, hi, i = pl.program_id(0), pl.program_id(1), pl.program_id(2)
    tq = q_ref.shape[0]
    tk = sa_ref.shape[0]
    s_len = k_ref.shape[0]
    q = q_ref[...]
    m_ref[...] = jnp.full_like(m_ref, NEG)
    acc_ref[...] = jnp.zeros_like(acc_ref)
    lane = lax.broadcasted_iota(jnp.int32, (1, HEAD_PAD), 1)

    @pl.when(i == 0)
    def _():
        def blk(r, mx):
            kf = k_ref[pl.ds(pl.multiple_of(r * NORM_ROWS, NORM_ROWS), NORM_ROWS), :].astype(F32)
            k2 = jnp.sum(jnp.where(lane < HEAD_DIM, kf * kf, 0.0), axis=1, keepdims=True)
            return jnp.maximum(mx, jnp.max(k2))
        kmax_ref[0] = jnp.sqrt(lax.fori_loop(0, s_len // NORM_ROWS, blk, jnp.float32(0.0)))

    qf = q.astype(F32)
    q2 = jnp.where(lane < HEAD_DIM, qf * qf, 0.0).astype(BF16)
    n2 = _nt_dot(jnp.ones((8, HEAD_PAD), BF16), q2)[0:1, :]
    reach = jnp.sqrt(n2) * (kmax_ref[0] * NORM_SLACK) + cq_ref[...]
    cbase = (bi * pl.num_programs(1) + hi) * (s_len // tk)

    def scores(j, s_ref, cm_ref, diag_off=None):
        ks = pl.multiple_of(jnp.maximum(j, 0) * tk, tk)
        st = _nt_dot(k_ref[pl.ds(ks, tk), :], q)
        if diag_off is not None:
            kp = lax.broadcasted_iota(jnp.int32, (tk, tq), 0) + diag_off
            qp = lax.broadcasted_iota(jnp.int32, (tk, tq), 1)
            st = jnp.where(kp <= qp, st, NEG)
        s_ref[...] = st
        cm_ref[...] = jnp.max(st, axis=0, keepdims=True)

    def consume(j, s_ref, cm_ref):
        ks = pl.multiple_of(j * tk, tk)
        vt = vt_ref[:, pl.ds(ks, tk)]
        m_prev = m_ref[...]
        m_new = jnp.maximum(m_prev, cm_ref[...])
        alpha = jnp.exp2(m_prev - m_new)
        pt = jnp.exp2(s_ref[...] - m_new).astype(BF16)
        acc_ref[...] = alpha * acc_ref[...] + _dot(vt, pt)
        m_ref[...] = m_new

    top = KPQ_C * i + 1
    late = tq - tk
    ks_top = pl.multiple_of(top * tk, tk)
    st_top = _nt_dot(k_ref[pl.ds(ks_top, tk), :], q[late:])
    kp = lax.broadcasted_iota(jnp.int32, (tk, tk), 0)
    qp = lax.broadcasted_iota(jnp.int32, (tk, tk), 1)
    st_top = jnp.where(kp <= qp, st_top, NEG)
    cm_top = jnp.max(st_top, axis=0, keepdims=True)
    pt_top = jnp.exp2(st_top - cm_top).astype(BF16)
    scores(top - 1, sb_ref, mb_ref, 0)
    acc_ref[:, late:] = _dot(vt_ref[:, pl.ds(ks_top, tk)], pt_top)
    m_ref[:, late:] = cm_top
    scores(top - 2, sa_ref, ma_ref)
    consume(top - 1, sb_ref, mb_ref)

    def cond(state):
        p, g = state
        return (p <= i) & (g - cend_ref[cbase + jnp.maximum(top - 2 * p, 0)] >= -FOX_EXIT)

    def pair(state):
        p, _ = state
        g = jnp.max(reach - m_ref[...])
        j = top - 2 * p
        scores(j - 1, sb_ref, mb_ref)
        consume(j, sa_ref, ma_ref)
        scores(j - 2, sa_ref, ma_ref)
        consume(j - 1, sb_ref, mb_ref)
        return p + 1, g

    lax.while_loop(cond, pair, (jnp.int32(1), jnp.max(reach - m_ref[...])))

    acc = acc_ref[...]
    out_t = acc * (1.0 / acc[HEAD_DIM:HEAD_DIM + 1, :])
    d_idx = lax.broadcasted_iota(jnp.int32, out_t.shape, 0)
    o_ref[...] = jnp.where(d_idx < HEAD_DIM, out_t, 0.0).T.astype(o_ref.dtype)


def _fox(q, k, vt, c_row):
    b, nh, s, _ = q.shape
    tq, tk = TQ_C, TK_C
    cend = c_row[:, :, 0, tk - 1::tk].reshape(-1)
    qblk = pl.BlockSpec((None, None, tq, HEAD_PAD), lambda bi, hi, i, ce: (bi, hi, i, 0))
    full = pl.BlockSpec((None, None, s, HEAD_PAD), lambda bi, hi, i, ce: (bi, hi, 0, 0))
    full_t = pl.BlockSpec((None, None, HEAD_PAD, s), lambda bi, hi, i, ce: (bi, hi, 0, 0))
    crow = pl.BlockSpec((None, None, 1, tq), lambda bi, hi, i, ce: (bi, hi, 0, i))
    return pl.pallas_call(
        _fox_kernel,
        grid_spec=pltpu.PrefetchScalarGridSpec(
            num_scalar_prefetch=1,
            grid=(b, nh, s // tq),
            in_specs=[qblk, full, full_t, crow],
            out_specs=pl.BlockSpec((None, tq, HEAD_PAD), lambda bi, hi, i, ce: (bi, i, hi)),
            scratch_shapes=[pltpu.VMEM((tk, tq), F32), pltpu.VMEM((tk, tq), F32),
                            pltpu.VMEM((1, tq), F32), pltpu.VMEM((1, tq), F32),
                            pltpu.VMEM((1, tq), F32), pltpu.VMEM((HEAD_PAD, tq), F32),
                            pltpu.SMEM((1,), F32)]),
        out_shape=jax.ShapeDtypeStruct((b, s, D_PAD), BF16),
        compiler_params=pltpu.CompilerParams(
            dimension_semantics=("arbitrary", "arbitrary", "arbitrary"),
            vmem_limit_bytes=VMEM_LIMIT),
        name="fox",
    )(cend, q, k, vt, c_row)


def _sb_kernel(q_ref, k_ref, v_ref, o_ref, r_ref, acc_ref):
    for sub in range(SUB_D):
        _sb_tile(pl.program_id(1) * SUB_D + sub, sub * TQ_D, q_ref, k_ref, v_ref, o_ref, r_ref, acc_ref)


def _sb_tile(i, q0, q_ref, k_ref, v_ref, o_ref, r_ref, acc_ref):
    nh, t = q_ref.shape[0], TQ_D
    row = lax.broadcasted_iota(jnp.int32, (t, t), 0)
    col = lax.broadcasted_iota(jnp.int32, (t, t), 1)
    tri = (row >= col).astype(BF16)
    tri2 = jnp.concatenate([tri, tri], axis=0)

    def tile(hh, kb, keep):
        ks = pl.multiple_of(kb * t, t)
        k = k_ref[hh, pl.ds(ks, t), :]
        v = v_ref[hh, pl.ds(ks, t), :]
        z = _nt_dot(q_ref[hh, q0:q0 + t, :], k)
        sp = jnp.maximum(z, 0.0) + jnp.log2(1.0 + jnp.exp2(-jnp.abs(z)))
        if keep is not None:
            sp = jnp.where(keep, sp, 0.0)
        hi = sp.astype(BF16)
        lo = (sp - hi.astype(F32)).astype(BF16)
        suf = _dot(jnp.concatenate([hi, lo], axis=1), tri2)
        return z, suf, v

    keep_a = col < row
    rmin = None
    for hh in range(nh):
        z_a, suf_a, v_a = tile(hh, i, keep_a)
        z_b, suf_b, v_b = tile(hh, jnp.maximum(i - 1, 0), None)
        v_b = jnp.where(i > 0, v_b, jnp.zeros_like(v_b))
        tot_a = suf_a[:, 0:1]
        a_a = jnp.where(keep_a, jnp.exp2(z_a - suf_a), 0.0)
        a_b = jnp.exp2(z_b - suf_b - tot_a)
        acc_ref[hh] = _dot(a_a.astype(BF16), v_a) + _dot(a_b.astype(BF16), v_b)
        r0 = tot_a + suf_b[:, 0:1]
        r_ref[hh] = r0
        rmin = jnp.min(r0) if rmin is None else jnp.minimum(rmin, jnp.min(r0))

    def cond(state):
        kb, rm = state
        return (kb >= 0) & (rm < SB_EXIT)

    def body(state):
        kb, _ = state
        rm = None
        for hh in range(nh):
            z, suf, v = tile(hh, kb, None)
            r = r_ref[hh]
            a = jnp.exp2(z - suf - r)
            acc_ref[hh] += _dot(a.astype(BF16), v)
            r = r + suf[:, 0:1]
            r_ref[hh] = r
            rm = jnp.min(r) if rm is None else jnp.minimum(rm, jnp.min(r))
        return kb - 1, rm

    lax.while_loop(cond, body, (i - 2, rmin))
    for hh in range(nh):
        o_ref[q0:q0 + t, hh * HEAD_PAD:(hh + 1) * HEAD_PAD] = acc_ref[hh].astype(o_ref.dtype)


def _stick_breaking(q, k, v):
    b, nh, s, _ = q.shape
    tq = SUB_D * TQ_D
    qblk = pl.BlockSpec((None, nh, tq, HEAD_PAD), lambda bi, i: (bi, 0, i, 0))
    full = pl.BlockSpec((None, nh, s, HEAD_PAD), lambda bi, i: (bi, 0, 0, 0),
                        pipeline_mode=pl.Buffered(1))
    return pl.pallas_call(
        _sb_kernel,
        grid=(b, s // tq),
        in_specs=[qblk, full, full],
        out_specs=pl.BlockSpec((None, tq, D_PAD), lambda bi, i: (bi, i, 0)),
        out_shape=jax.ShapeDtypeStruct((b, s, D_PAD), BF16),
        scratch_shapes=[pltpu.VMEM((nh, TQ_D, 1), F32), pltpu.VMEM((nh, TQ_D, HEAD_PAD), F32)],
        compiler_params=pltpu.CompilerParams(
            dimension_semantics=("arbitrary", "arbitrary"),
            vmem_limit_bytes=VMEM_LIMIT),
        name="stick_breaking",
    )(q, k, v)


def _out_kernel(x_ref, ya_ref, ga_ref, pb_ref, yc_ref, gc_ref, yd_ref, gd_ref,
                bga_ref, bgb_ref, bgc_ref, bgd_ref, vg_ref, ws_ref, bs_ref,
                woa_ref, wob_ref, woc_ref, wod_ref, fg_ref, o_ref, *, last):
    tm = x_ref.shape[0]

    def gated(y, g_ref, gain_ref):
        ms = jnp.sum(y * y, axis=-1, keepdims=True) * (1.0 / D_BRANCH)
        g = g_ref[...].astype(F32)
        silu = g * (1.0 / (1.0 + jnp.exp(-g)))
        return (y * lax.rsqrt(ms + EPS) * gain_ref[...] * silu).astype(BF16)

    u = pb_ref[:, 0:D_BRANCH].astype(F32)
    vb = pb_ref[:, D_BRANCH:2 * D_BRANCH].astype(F32)
    mu = jnp.mean(vb, axis=-1, keepdims=True)
    xc = vb - mu
    var = jnp.mean(xc * xc, axis=-1, keepdims=True)
    vn = (xc * lax.rsqrt(var + EPS) * vg_ref[...]).astype(BF16)
    row = lax.broadcasted_iota(jnp.int32, (SG_CHUNK, SG_CHUNK), 0)
    col = lax.broadcasted_iota(jnp.int32, (SG_CHUNK, SG_CHUNK), 1)
    group_of_lane = lax.broadcasted_iota(jnp.int32, (1, D_BRANCH), 1) // HEAD_DIM
    ws = [jnp.where(row >= col, ws_ref[g], 0.0).astype(BF16) for g in range(N_HEADS)]
    mixed = []
    for ci in range(tm // SG_CHUNK):
        vchunk = vn[ci * SG_CHUNK:(ci + 1) * SG_CHUNK]
        acc = bs_ref[...]
        for g in range(N_HEADS):
            acc = acc + jnp.where(group_of_lane == g, _dot(ws[g], vchunk), 0.0)
        mixed.append(acc)
    yb = u * jnp.concatenate(mixed, axis=0)

    out = x_ref[...]
    out = out + _dot(gated(ya_ref[...].astype(F32), ga_ref, bga_ref), woa_ref[...])
    out = out + _dot(gated(yb, pb_ref.at[:, 2 * D_BRANCH:3 * D_BRANCH], bgb_ref), wob_ref[...])
    lane = lax.broadcasted_iota(jnp.int32, (1, HEAD_PAD), 1)

    def unpad(y_ref):
        halves = []
        for pr in range(N_HEADS // 2):
            even = y_ref[:, (2 * pr) * HEAD_PAD:(2 * pr + 1) * HEAD_PAD].astype(F32)
            odd = y_ref[:, (2 * pr + 1) * HEAD_PAD:(2 * pr + 2) * HEAD_PAD].astype(F32)
            halves.append(jnp.where(lane < HEAD_DIM, even, pltpu.roll(odd, HEAD_DIM, 1)))
        return jnp.concatenate(halves, axis=1)

    out = out + _dot(gated(unpad(yc_ref), gc_ref, bgc_ref), woc_ref[...])
    out = out + _dot(gated(unpad(yd_ref), gd_ref, bgd_ref), wod_ref[...])
    if last:
        ms = jnp.mean(out * out, axis=-1, keepdims=True)
        out = out * lax.rsqrt(ms + EPS) * fg_ref[...]
    o_ref[...] = out


def _out_proj(x, ya, pa, pb, yc, gc, yd, gd, bga, bgb, bgc, bgd, vg, ws, bs,
              woa, wob, woc, wod, fg, last):
    b, s, d = x.shape
    tm = TM_OUT
    tok = lambda n, colblk=0: pl.BlockSpec((None, tm, n), lambda bi, i: (bi, i, colblk))
    const = lambda a: pl.BlockSpec(a.shape, lambda bi, i: (0,) * a.ndim)
    return pl.pallas_call(
        functools.partial(_out_kernel, last=last),
        grid=(b, s // tm),
        in_specs=[tok(d), tok(D_BRANCH), tok(D_BRANCH, 3), tok(3 * D_BRANCH), tok(D_PAD), tok(D_BRANCH),
                  tok(D_PAD), tok(D_BRANCH), const(bga), const(bgb), const(bgc), const(bgd), const(vg),
                  const(ws), const(bs), const(woa), const(wob), const(woc), const(wod), const(fg)],
        out_specs=tok(d),
        out_shape=jax.ShapeDtypeStruct((b, s, d), F32),
        compiler_params=pltpu.CompilerParams(
            dimension_semantics=("arbitrary", "arbitrary"), vmem_limit_bytes=VMEM_LIMIT),
        name="out_proj",
    )(x, ya, pa, pb, yc, gc, yd, gd, bga, bgb, bgc, bgd, vg, ws, bs, woa, wob, woc, wod, fg)


def _layer(x, norm_g, w_in, b_f, rel_bias, w_s, b_s, v_gain, branch_gain, w_out, final_g, last):
    db = D_BRANCH
    o = 0
    wa = w_in[:, o:o + 4 * db]; o += 4 * db
    wb = w_in[:, o:o + 3 * db]; o += 3 * db
    wc = w_in[:, o:o + 4 * db]; o += 4 * db
    wf = w_in[:, o:o + N_HEADS]; o += N_HEADS
    wd = w_in[:, o:o + 4 * db]

    def q_scaled(w):
        return jnp.concatenate([w[:, :db] * SCALE, w[:, db:]], axis=1).astype(BF16)

    wf_p = jnp.pad(wf, ((0, 0), (0, HEAD_PAD - N_HEADS))).astype(BF16)
    bf_p = jnp.pad(b_f, (0, HEAD_PAD - N_HEADS)).reshape(1, HEAD_PAD)

    pa, pb, qc, kc, vc, gc, c2, qd, kd, vd, gd = _inproj(
        x, norm_g.reshape(1, -1), q_scaled(wa), wb.astype(BF16), q_scaled(wc), wf_p, q_scaled(wd), bf_p)

    ya = _chunk_attn(pa, _chunk_bias_row(rel_bias))
    yc = _fox(qc, kc, vc, c2[:, :N_HEADS, None, :])
    yd = _stick_breaking(qd, kd, vd)

    bs_tile = jnp.repeat(jnp.transpose(b_s), HEAD_DIM, axis=1)
    return _out_proj(
        x, ya, pa, pb, yc, gc, yd, gd,
        branch_gain[0].reshape(1, -1), branch_gain[1].reshape(1, -1),
        branch_gain[2].reshape(1, -1), branch_gain[3].reshape(1, -1),
        v_gain.reshape(1, -1), w_s, bs_tile,
        w_out[0:db].astype(BF16), w_out[db:2 * db].astype(BF16),
        w_out[2 * db:3 * db].astype(BF16), w_out[3 * db:4 * db].astype(BF16),
        final_g.reshape(1, -1), last)


def kernel(x, norm_g, w_in, b_f, rel_bias, w_s, b_s, v_gain, branch_gain, w_out, final_g):
    depth = norm_g.shape[0]
    for l in range(depth):
        x = _layer(x, norm_g[l], w_in[l], b_f[l], rel_bias[l], w_s[l], b_s[l], v_gain[l],
                   branch_gain[l], w_out[l], final_g, last=(l == depth - 1))
    return x
```

```python
import functools

import jax
import jax.numpy as jnp
import numpy as np
from jax import lax
from jax.experimental import pallas as pl
from jax.experimental.pallas import tpu as pltpu

D_MODEL = 1024
N_HEADS = 4
HEAD_DIM = 64
D_BRANCH = N_HEADS * HEAD_DIM
CHUNK = 64
LOOKBACK_CHUNKS = 8
MAX_REL = 128
SG_CHUNK = 128
EPS = 1e-6
SCALE = HEAD_DIM ** -0.5
HEAD_PAD = 128
D_PAD = N_HEADS * HEAD_PAD
NEG = -1e30
LOG2E = 1.4426950408889634
SB_EXIT = 151.0
FOX_EXIT = 152.0
NORM_SLACK = 1.01
NORM_ROWS = 2048

F32 = jnp.float32
BF16 = jnp.bfloat16

TM_IN = 512
TQ_A = 256
LOOK_A = LOOKBACK_CHUNKS * CHUNK
NBLK_A = LOOK_A // TQ_A + 1
WIN_A = LOOK_A + TQ_A
ROLL_A = 1024
TQ_C = 1024
TK_C = 512
KPQ_C = TQ_C // TK_C
SUB_C = 2
TQ_D = 256
SUB_D = 2
TM_OUT = 512
VMEM_LIMIT = 56 * 1024 * 1024


def _nt_dot(a, b):
    return lax.dot_general(a, b, (((1,), (1,)), ((), ())), preferred_element_type=F32)


def _dot(a, b):
    return jnp.dot(a, b, preferred_element_type=F32)


def _bf16_part(x):
    return x.astype(BF16).astype(F32)


def _split3(x):
    hi = _bf16_part(x)
    mid = _bf16_part(x - hi)
    return hi, mid, x - hi - mid


def _inproj_kernel(x_ref, g_ref, wa_ref, wb_ref, wc_ref, wf_ref, wd_ref, bf_ref,
                   pa_ref, pb_ref, qc_ref, kc_ref, vc_ref, gc_ref, c_ref,
                   qd_ref, kd_ref, vd_ref, gd_ref, carry_ref):
    i = pl.program_id(1)
    tm = x_ref.shape[0]
    x = x_ref[...]
    ms = jnp.mean(x * x, axis=-1, keepdims=True)
    h = (x * lax.rsqrt(ms + EPS) * g_ref[...]).astype(BF16)

    pa = _dot(h, wa_ref[...])
    pa_ref[:, :D_BRANCH] = (pa[:, :D_BRANCH] * LOG2E).astype(BF16)
    pa_ref[:, D_BRANCH:] = pa[:, D_BRANCH:].astype(BF16)
    pb_ref[...] = _dot(h, wb_ref[...]).astype(BF16)

    f = _dot(h, wf_ref[...]) + bf_ref[...]
    ls = jnp.minimum(f, 0.0) - jnp.log(1.0 + jnp.exp(-jnp.abs(f)))
    row = lax.broadcasted_iota(jnp.int32, (tm, tm), 0)
    col = lax.broadcasted_iota(jnp.int32, (tm, tm), 1)
    tri = (row >= col).astype(BF16)
    ls_hi, ls_mid, ls_lo = _split3(ls)
    parts = _dot(tri, jnp.concatenate([ls_hi, ls_mid, ls_lo], axis=1).astype(BF16))
    local = parts[:, :HEAD_PAD] + parts[:, HEAD_PAD:2 * HEAD_PAD] + parts[:, 2 * HEAD_PAD:]

    @pl.when(i == 0)
    def _():
        carry_ref[...] = jnp.zeros_like(carry_ref)

    c = local + carry_ref[...]
    carry_ref[...] = c[tm - 1:tm, :]
    c2 = c * LOG2E
    c_ref[...] = c2.T

    lane = lax.broadcasted_iota(jnp.int32, (1, HEAD_PAD), 1)
    a0 = HEAD_DIM
    k_const = ((lane >= a0) & (lane < a0 + 3)).astype(F32)
    q_const = -((lane >= a0 + 3) & (lane < a0 + 6)).astype(F32)
    v_const = (lane == a0).astype(F32)

    def head_tile(p, part, hh, spare, scale=None, transpose=False):
        lo = part * D_BRANCH + (hh // 2) * HEAD_PAD
        tile = p[:, lo:lo + HEAD_PAD]
        if hh % 2:
            tile = pltpu.roll(tile, HEAD_DIM, 1)
        if scale is not None:
            tile = tile * scale
        tile = jnp.where(lane < HEAD_DIM, tile, spare)
        return (tile.T if transpose else tile).astype(BF16)

    pc = _dot(h, wc_ref[...])
    pieces = _split3(c2)
    for hh in range(N_HEADS):
        hi, mid, lo = (jnp.broadcast_to(p[:, hh:hh + 1], (tm, HEAD_PAD)) for p in pieces)
        q_spare = jnp.where(lane == a0, hi, jnp.where(lane == a0 + 1, mid, jnp.where(lane == a0 + 2, lo, q_const)))
        k_spare = jnp.where(lane == a0 + 3, hi, jnp.where(lane == a0 + 4, mid, jnp.where(lane == a0 + 5, lo, k_const)))
        qc_ref[hh] = head_tile(pc, 0, hh, q_spare, LOG2E)
        kc_ref[hh] = head_tile(pc, 1, hh, k_spare)
        vc_ref[hh] = head_tile(pc, 2, hh, v_const, transpose=True)
    gc_ref[...] = pc[:, 3 * D_BRANCH:4 * D_BRANCH].astype(BF16)

    pd = _dot(h, wd_ref[...])
    for hh in range(N_HEADS):
        qd_ref[hh] = head_tile(pd, 0, hh, 0.0, LOG2E)
        kd_ref[hh] = head_tile(pd, 1, hh, 0.0)
        vd_ref[hh] = head_tile(pd, 2, hh, 0.0)
    gd_ref[...] = pd[:, 3 * D_BRANCH:4 * D_BRANCH].astype(BF16)


def _inproj(x, norm_g, wa, wb, wc, wf, wd, bf):
    b, s, d = x.shape
    tm = TM_IN
    const = lambda shape: pl.BlockSpec(shape, lambda bi, i: (0,) * len(shape))
    tok = lambda n: pl.BlockSpec((None, tm, n), lambda bi, i: (bi, i, 0))
    head = pl.BlockSpec((None, N_HEADS, tm, HEAD_PAD), lambda bi, i: (bi, 0, i, 0))
    head_t = pl.BlockSpec((None, N_HEADS, HEAD_PAD, tm), lambda bi, i: (bi, 0, 0, i))
    head_shape = jax.ShapeDtypeStruct((b, N_HEADS, s, HEAD_PAD), BF16)
    return pl.pallas_call(
        _inproj_kernel,
        grid=(b, s // tm),
        in_specs=[tok(d), const((1, d)), const(wa.shape), const(wb.shape), const(wc.shape),
                  const(wf.shape), const(wd.shape), const((1, HEAD_PAD))],
        out_specs=[tok(wa.shape[1]), tok(wb.shape[1]), head, head, head_t, tok(D_BRANCH),
                   pl.BlockSpec((None, HEAD_PAD, tm), lambda bi, i: (bi, 0, i)),
                   head, head, head, tok(D_BRANCH)],
        out_shape=[jax.ShapeDtypeStruct((b, s, wa.shape[1]), BF16),
                   jax.ShapeDtypeStruct((b, s, wb.shape[1]), BF16),
                   head_shape, head_shape,
                   jax.ShapeDtypeStruct((b, N_HEADS, HEAD_PAD, s), BF16),
                   jax.ShapeDtypeStruct((b, s, D_BRANCH), BF16),
                   jax.ShapeDtypeStruct((b, HEAD_PAD, s), F32),
                   head_shape, head_shape, head_shape,
                   jax.ShapeDtypeStruct((b, s, D_BRANCH), BF16)],
        scratch_shapes=[pltpu.VMEM((1, HEAD_PAD), F32)],
        compiler_params=pltpu.CompilerParams(
            dimension_semantics=("arbitrary", "arbitrary"), vmem_limit_bytes=VMEM_LIMIT),
        name="inproj",
    )(x, norm_g, wa, wb, wc, wf, wd, bf)


def _chunk_attn_kernel(q_ref, *refs):
    k_refs, v_refs = refs[:NBLK_A], refs[NBLK_A:2 * NBLK_A]
    rb_ref, o_ref, bias_ref = refs[2 * NBLK_A:]
    i = pl.program_id(1)
    tq = q_ref.shape[0]

    @pl.when((pl.program_id(0) == 0) & (i == 0))
    def _():
        ti = lax.broadcasted_iota(jnp.int32, (tq, WIN_A), 0) // CHUNK
        sj = lax.broadcasted_iota(jnp.int32, (tq, WIN_A), 1) // CHUNK
        band = (sj >= ti) & (sj <= ti + LOOKBACK_CHUNKS)
        for hh in range(N_HEADS):
            rows = jnp.broadcast_to(rb_ref[hh:hh + 1, :], (tq, ROLL_A))
            toep = pltpu.roll(rows, 0, 1, stride=1, stride_axis=0)
            bias_ref[hh] = jnp.where(band, toep[:, :WIN_A], NEG)

    q = q_ref[...]
    k = jnp.concatenate([r[...] for r in k_refs], axis=0)
    v = jnp.concatenate([r[...] for r in v_refs], axis=0)
    head_of_lane = lax.broadcasted_iota(jnp.int32, (1, D_BRANCH), 1) // HEAD_DIM
    col = lax.broadcasted_iota(jnp.int32, (tq, WIN_A), 1)
    in_seq = col >= LOOK_A - i * tq
    out = jnp.zeros((tq, D_BRANCH), F32)
    for hh in range(N_HEADS):
        sel = head_of_lane == hh
        qh = jnp.where(sel, q, jnp.zeros_like(q))
        s = _nt_dot(qh, k) + bias_ref[hh]
        s = jnp.where(in_seq, s, NEG)
        m = jnp.max(s, axis=-1, keepdims=True)
        p = jnp.exp2(s - m)
        l = jnp.sum(p, axis=-1, keepdims=True)
        pv = _dot(p.astype(BF16), v)
        out = out + jnp.where(sel, pv * (1.0 / l), 0.0)
    o_ref[...] = out.astype(o_ref.dtype)


def _chunk_attn(pa, rb_row):
    b, s, _ = pa.shape
    tq = TQ_A
    blk = lambda colblk, back: pl.BlockSpec(
        (None, tq, D_BRANCH), lambda bi, i: (bi, jnp.maximum(i - back, 0), colblk))
    return pl.pallas_call(
        _chunk_attn_kernel,
        grid=(b, s // tq),
        in_specs=([blk(0, 0)] + [blk(1, back) for back in reversed(range(NBLK_A))]
                  + [blk(2, back) for back in reversed(range(NBLK_A))]
                  + [pl.BlockSpec(rb_row.shape, lambda bi, i: (0, 0))]),
        out_specs=pl.BlockSpec((None, tq, D_BRANCH), lambda bi, i: (bi, i, 0)),
        out_shape=jax.ShapeDtypeStruct((b, s, D_BRANCH), BF16),
        scratch_shapes=[pltpu.VMEM((N_HEADS, tq, WIN_A), F32)],
        compiler_params=pltpu.CompilerParams(
            dimension_semantics=("arbitrary", "arbitrary"), vmem_limit_bytes=VMEM_LIMIT),
        name="chunk_attn",
    )(*([pa] * (1 + 2 * NBLK_A)), rb_row)


def _chunk_bias_row(rel_bias):
    u = np.arange(ROLL_A)
    e = np.where(u < WIN_A, u, u - ROLL_A)
    rel = np.clip(LOOK_A - e, -MAX_REL, MAX_REL) + MAX_REL
    return rel_bias[:, rel].astype(F32) * LOG2E


def _fox_kernel(cend_ref, q_ref, k_ref, vt_ref, cq_ref, o_ref, *scratch):
    kmax_ref = scratch[-1]
    lane = lax.broadcasted_iota(jnp.int32, (1, HEAD_PAD), 1)

    @pl.when(pl.program_id(2) == 0)
    def _():
        def blk(r, mx):
            kf = k_ref[pl.ds(pl.multiple_of(r * NORM_ROWS, NORM_ROWS), NORM_ROWS), :].astype(F32)
            k2 = jnp.sum(jnp.where(lane < HEAD_DIM, kf * kf, 0.0), axis=1, keepdims=True)
            return jnp.maximum(mx, jnp.max(k2))
        kmax_ref[0] = jnp.sqrt(lax.fori_loop(0, k_ref.shape[0] // NORM_ROWS, blk, jnp.float32(0.0)))

    for sub in range(SUB_C):
        _fox_tile(pl.program_id(2) * SUB_C + sub, sub * TQ_C, cend_ref,
                  q_ref, k_ref, vt_ref, cq_ref, o_ref, *scratch)


def _fox_tile(i, q0, cend_ref, q_ref, k_ref, vt_ref, cq_ref, o_ref,
              sa_ref, sb_ref, ma_ref, mb_ref, m_ref, acc_ref, kmax_ref):
    bi, hi = pl.program_id(0), pl.program_id(1)
    tq = TQ_C
    tk = sa_ref.shape[0]
    s_len = k_ref.shape[0]
    q = q_ref[q0:q0 + tq, :]
    m_ref[...] = jnp.full_like(m_ref, NEG)
    acc_ref[...] = jnp.zeros_like(acc_ref)
    lane = lax.broadcasted_iota(jnp.int32, (1, HEAD_PAD), 1)

    qf = q.astype(F32)
    q2 = jnp.where(lane < HEAD_DIM, qf * qf, 0.0).astype(BF16)
    n2 = _nt_dot(jnp.ones((8, HEAD_PAD), BF16), q2)[0:1, :]
    reach = jnp.sqrt(n2) * (kmax_ref[0] * NORM_SLACK) + cq_ref[:, q0:q0 + tq]
    cbase = (bi * pl.num_programs(1) + hi) * (s_len // tk)

    def scores(j, s_ref, cm_ref, diag_off=None):
        ks = pl.multiple_of(jnp.maximum(j, 0) * tk, tk)
        st = _nt_dot(k_ref[pl.ds(ks, tk), :], q)
        if diag_off is not None:
            kp = lax.broadcasted_iota(jnp.int32, (tk, tq), 0) + diag_off
            qp = lax.broadcasted_iota(jnp.int32, (tk, tq), 1)
            st = jnp.where(kp <= qp, st, NEG)
        s_ref[...] = st
        cm_ref[...] = jnp.max(st, axis=0, keepdims=True)

    def consume(j, s_ref, cm_ref):
        ks = pl.multiple_of(j * tk, tk)
        vt = vt_ref[:, pl.ds(ks, tk)]
        m_prev = m_ref[...]
        m_new = jnp.maximum(m_prev, cm_ref[...])
        alpha = jnp.exp2(m_prev - m_new)
        pt = jnp.exp2(s_ref[...] - m_new).astype(BF16)
        acc_ref[...] = alpha * acc_ref[...] + _dot(vt, pt)
        m_ref[...] = m_new

    top = KPQ_C * i + 1
    late = tq - tk
    ks_top = pl.multiple_of(top * tk, tk)
    st_top = _nt_dot(k_ref[pl.ds(ks_top, tk), :], q[late:])
    kp = lax.broadcasted_iota(jnp.int32, (tk, tk), 0)
    qp = lax.broadcasted_iota(jnp.int32, (tk, tk), 1)
    st_top = jnp.where(kp <= qp, st_top, NEG)
    cm_top = jnp.max(st_top, axis=0, keepdims=True)
    pt_top = jnp.exp2(st_top - cm_top).astype(BF16)
    scores(top - 1, sb_ref, mb_ref, 0)
    acc_ref[:, late:] = _dot(vt_ref[:, pl.ds(ks_top, tk)], pt_top)
    m_ref[:, late:] = cm_top
    scores(top - 2, sa_ref, ma_ref)
    consume(top - 1, sb_ref, mb_ref)

    def cond(state):
        p, g = state
        return (p <= i) & (g - cend_ref[cbase + jnp.maximum(top - 2 * p, 0)] >= -FOX_EXIT)

    def pair(state):
        p, _ = state
        g = jnp.max(reach - m_ref[...])
        j = top - 2 * p
        scores(j - 1, sb_ref, mb_ref)
        consume(j, sa_ref, ma_ref)
        scores(j - 2, sa_ref, ma_ref)
        consume(j - 1, sb_ref, mb_ref)
        return p + 1, g

    lax.while_loop(cond, pair, (jnp.int32(1), jnp.max(reach - m_ref[...])))

    acc = acc_ref[...]
    out_t = acc * (1.0 / acc[HEAD_DIM:HEAD_DIM + 1, :])
    d_idx = lax.broadcasted_iota(jnp.int32, out_t.shape, 0)
    o_ref[q0:q0 + tq, :] = jnp.where(d_idx < HEAD_DIM, out_t, 0.0).T.astype(o_ref.dtype)


def _fox(q, k, vt, c_row):
    b, nh, s, _ = q.shape
    tq, tk = TQ_C, TK_C
    step = SUB_C * tq
    cend = c_row[:, :, 0, tk - 1::tk].reshape(-1)
    qblk = pl.BlockSpec((None, None, step, HEAD_PAD), lambda bi, hi, i, ce: (bi, hi, i, 0))
    full = pl.BlockSpec((None, None, s, HEAD_PAD), lambda bi, hi, i, ce: (bi, hi, 0, 0))
    full_t = pl.BlockSpec((None, None, HEAD_PAD, s), lambda bi, hi, i, ce: (bi, hi, 0, 0))
    crow = pl.BlockSpec((None, None, 1, step), lambda bi, hi, i, ce: (bi, hi, 0, i))
    return pl.pallas_call(
        _fox_kernel,
        grid_spec=pltpu.PrefetchScalarGridSpec(
            num_scalar_prefetch=1,
            grid=(b, nh, s // step),
            in_specs=[qblk, full, full_t, crow],
            out_specs=pl.BlockSpec((None, step, HEAD_PAD), lambda bi, hi, i, ce: (bi, i, hi)),
            scratch_shapes=[pltpu.VMEM((tk, tq), F32), pltpu.VMEM((tk, tq), F32),
                            pltpu.VMEM((1, tq), F32), pltpu.VMEM((1, tq), F32),
                            pltpu.VMEM((1, tq), F32), pltpu.VMEM((HEAD_PAD, tq), F32),
                            pltpu.SMEM((1,), F32)]),
        out_shape=jax.ShapeDtypeStruct((b, s, D_PAD), BF16),
        compiler_params=pltpu.CompilerParams(
            dimension_semantics=("arbitrary", "arbitrary", "arbitrary"),
            vmem_limit_bytes=VMEM_LIMIT),
        name="fox",
    )(cend, q, k, vt, c_row)


def _sb_kernel(q_ref, k_ref, v_ref, o_ref, r_ref, acc_ref):
    for sub in range(SUB_D):
        _sb_tile(pl.program_id(1) * SUB_D + sub, sub * TQ_D, q_ref, k_ref, v_ref, o_ref, r_ref, acc_ref)


def _sb_tile(i, q0, q_ref, k_ref, v_ref, o_ref, r_ref, acc_ref):
    nh, t = q_ref.shape[0], TQ_D
    row = lax.broadcasted_iota(jnp.int32, (t, t), 0)
    col = lax.broadcasted_iota(jnp.int32, (t, t), 1)
    tri = (row >= col).astype(BF16)
    tri2 = jnp.concatenate([tri, tri], axis=0)

    def tile(hh, kb, keep):
        ks = pl.multiple_of(kb * t, t)
        k = k_ref[hh, pl.ds(ks, t), :]
        v = v_ref[hh, pl.ds(ks, t), :]
        z = _nt_dot(q_ref[hh, q0:q0 + t, :], k)
        sp = jnp.maximum(z, 0.0) + jnp.log2(1.0 + jnp.exp2(-jnp.abs(z)))
        if keep is not None:
            sp = jnp.where(keep, sp, 0.0)
        hi = sp.astype(BF16)
        lo = (sp - hi.astype(F32)).astype(BF16)
        suf = _dot(jnp.concatenate([hi, lo], axis=1), tri2)
        return z, suf, v

    keep_a = col < row
    rmin = None
    for hh in range(nh):
        z_a, suf_a, v_a = tile(hh, i, keep_a)
        z_b, suf_b, v_b = tile(hh, jnp.maximum(i - 1, 0), None)
        v_b = jnp.where(i > 0, v_b, jnp.zeros_like(v_b))
        tot_a = suf_a[:, 0:1]
        a_a = jnp.where(keep_a, jnp.exp2(z_a - suf_a), 0.0)
        a_b = jnp.exp2(z_b - suf_b - tot_a)
        acc_ref[hh] = _dot(a_a.astype(BF16), v_a) + _dot(a_b.astype(BF16), v_b)
        r0 = tot_a + suf_b[:, 0:1]
        r_ref[hh] = r0
        rmin = jnp.min(r0) if rmin is None else jnp.minimum(rmin, jnp.min(r0))

    def cond(state):
        kb, rm = state
        return (kb >= 0) & (rm < SB_EXIT)

    def body(state):
        kb, _ = state
        rm = None
        for hh in range(nh):
            z, suf, v = tile(hh, kb, None)
            r = r_ref[hh]
            a = jnp.exp2(z - suf - r)
            acc_ref[hh] += _dot(a.astype(BF16), v)
            r = r + suf[:, 0:1]
            r_ref[hh] = r
            rm = jnp.min(r) if rm is None else jnp.minimum(rm, jnp.min(r))
        return kb - 1, rm

    lax.while_loop(cond, body, (i - 2, rmin))
    for hh in range(nh):
        o_ref[q0:q0 + t, hh * HEAD_PAD:(hh + 1) * HEAD_PAD] = acc_ref[hh].astype(o_ref.dtype)


def _stick_breaking(q, k, v):
    b, nh, s, _ = q.shape
    tq = SUB_D * TQ_D
    qblk = pl.BlockSpec((None, nh, tq, HEAD_PAD), lambda bi, i: (bi, 0, i, 0))
    full = pl.BlockSpec((None, nh, s, HEAD_PAD), lambda bi, i: (bi, 0, 0, 0),
                        pipeline_mode=pl.Buffered(1))
    return pl.pallas_call(
        _sb_kernel,
        grid=(b, s // tq),
        in_specs=[qblk, full, full],
        out_specs=pl.BlockSpec((None, tq, D_PAD), lambda bi, i: (bi, i, 0)),
        out_shape=jax.ShapeDtypeStruct((b, s, D_PAD), BF16),
        scratch_shapes=[pltpu.VMEM((nh, TQ_D, 1), F32), pltpu.VMEM((nh, TQ_D, HEAD_PAD), F32)],
        compiler_params=pltpu.CompilerParams(
            dimension_semantics=("arbitrary", "arbitrary"),
            vmem_limit_bytes=VMEM_LIMIT),
        name="stick_breaking",
    )(q, k, v)


def _out_kernel(x_ref, ya_ref, ga_ref, pb_ref, yc_ref, gc_ref, yd_ref, gd_ref,
                bga_ref, bgb_ref, bgc_ref, bgd_ref, vg_ref, ws_ref, bs_ref,
                wo_ref, fg_ref, o_ref, *, last):
    tm = x_ref.shape[0]

    def gated(y, g_ref, gain_ref):
        ms = jnp.sum(y * y, axis=-1, keepdims=True) * (1.0 / D_BRANCH)
        g = g_ref[...].astype(F32)
        silu = g * (1.0 / (1.0 + jnp.exp(-g)))
        return (y * lax.rsqrt(ms + EPS) * gain_ref[...] * silu).astype(BF16)

    u = pb_ref[:, 0:D_BRANCH].astype(F32)
    vb = pb_ref[:, D_BRANCH:2 * D_BRANCH].astype(F32)
    mu = jnp.mean(vb, axis=-1, keepdims=True)
    xc = vb - mu
    var = jnp.mean(xc * xc, axis=-1, keepdims=True)
    vn = (xc * lax.rsqrt(var + EPS) * vg_ref[...]).astype(BF16)
    row = lax.broadcasted_iota(jnp.int32, (SG_CHUNK, SG_CHUNK), 0)
    col = lax.broadcasted_iota(jnp.int32, (SG_CHUNK, SG_CHUNK), 1)
    group_of_lane = lax.broadcasted_iota(jnp.int32, (1, D_BRANCH), 1) // HEAD_DIM
    ws = [jnp.where(row >= col, ws_ref[g], 0.0).astype(BF16) for g in range(N_HEADS)]
    mixed = []
    for ci in range(tm // SG_CHUNK):
        vchunk = vn[ci * SG_CHUNK:(ci + 1) * SG_CHUNK]
        acc = bs_ref[...]
        for g in range(N_HEADS):
            acc = acc + jnp.where(group_of_lane == g, _dot(ws[g], vchunk), 0.0)
        mixed.append(acc)
    yb = u * jnp.concatenate(mixed, axis=0)

    def wo(j):
        return wo_ref[j * D_BRANCH:(j + 1) * D_BRANCH, :]

    out = x_ref[...]
    out = out + _dot(gated(ya_ref[...].astype(F32), ga_ref, bga_ref), wo(0))
    out = out + _dot(gated(yb, pb_ref.at[:, 2 * D_BRANCH:3 * D_BRANCH], bgb_ref), wo(1))
    lane = lax.broadcasted_iota(jnp.int32, (1, HEAD_PAD), 1)

    def unpad(y_ref):
        halves = []
        for pr in range(N_HEADS // 2):
            even = y_ref[:, (2 * pr) * HEAD_PAD:(2 * pr + 1) * HEAD_PAD].astype(F32)
            odd = y_ref[:, (2 * pr + 1) * HEAD_PAD:(2 * pr + 2) * HEAD_PAD].astype(F32)
            halves.append(jnp.where(lane < HEAD_DIM, even, pltpu.roll(odd, HEAD_DIM, 1)))
        return jnp.concatenate(halves, axis=1)

    out = out + _dot(gated(unpad(yc_ref), gc_ref, bgc_ref), wo(2))
    out = out + _dot(gated(unpad(yd_ref), gd_ref, bgd_ref), wo(3))
    if last:
        ms = jnp.mean(out * out, axis=-1, keepdims=True)
        out = out * lax.rsqrt(ms + EPS) * fg_ref[...]
    o_ref[...] = out


def _out_proj(x, ya, pa, pb, yc, gc, yd, gd, bga, bgb, bgc, bgd, vg, ws, bs,
              wo, fg, last):
    b, s, d = x.shape
    tm = TM_OUT
    tok = lambda n, colblk=0: pl.BlockSpec((None, tm, n), lambda bi, i: (bi, i, colblk))
    const = lambda a: pl.BlockSpec(a.shape, lambda bi, i: (0,) * a.ndim)
    return pl.pallas_call(
        functools.partial(_out_kernel, last=last),
        grid=(b, s // tm),
        in_specs=[tok(d), tok(D_BRANCH), tok(D_BRANCH, 3), tok(3 * D_BRANCH), tok(D_PAD), tok(D_BRANCH),
                  tok(D_PAD), tok(D_BRANCH), const(bga), const(bgb), const(bgc), const(bgd), const(vg),
                  const(ws), const(bs), const(wo), const(fg)],
        out_specs=tok(d),
        out_shape=jax.ShapeDtypeStruct((b, s, d), F32),
        compiler_params=pltpu.CompilerParams(
            dimension_semantics=("arbitrary", "arbitrary"), vmem_limit_bytes=VMEM_LIMIT),
        name="out_proj",
    )(x, ya, pa, pb, yc, gc, yd, gd, bga, bgb, bgc, bgd, vg, ws, bs, wo, fg)


def _layer(x, norm_g, w_in, b_f, rel_bias, w_s, b_s, v_gain, branch_gain, w_out, final_g, last):
    db = D_BRANCH
    o = 0
    wa = w_in[:, o:o + 4 * db]; o += 4 * db
    wb = w_in[:, o:o + 3 * db]; o += 3 * db
    wc = w_in[:, o:o + 4 * db]; o += 4 * db
    wf = w_in[:, o:o + N_HEADS]; o += N_HEADS
    wd = w_in[:, o:o + 4 * db]

    def q_scaled(w):
        return jnp.concatenate([w[:, :db] * SCALE, w[:, db:]], axis=1).astype(BF16)

    wf_p = jnp.pad(wf, ((0, 0), (0, HEAD_PAD - N_HEADS))).astype(BF16)
    bf_p = jnp.pad(b_f, (0, HEAD_PAD - N_HEADS)).reshape(1, HEAD_PAD)

    pa, pb, qc, kc, vc, gc, c2, qd, kd, vd, gd = _inproj(
        x, norm_g.reshape(1, -1), q_scaled(wa), wb.astype(BF16), q_scaled(wc), wf_p, q_scaled(wd), bf_p)

    ya = _chunk_attn(pa, _chunk_bias_row(rel_bias))
    yc = _fox(qc, kc, vc, c2[:, :N_HEADS, None, :])
    yd = _stick_breaking(qd, kd, vd)

    bs_tile = jnp.repeat(jnp.transpose(b_s), HEAD_DIM, axis=1)
    return _out_proj(
        x, ya, pa, pb, yc, gc, yd, gd,
        branch_gain[0].reshape(1, -1), branch_gain[1].reshape(1, -1),
        branch_gain[2].reshape(1, -1), branch_gain[3].reshape(1, -1),
        v_gain.reshape(1, -1), w_s, bs_tile,
        w_out.astype(BF16), final_g.reshape(1, -1), last)


def kernel(x, norm_g, w_in, b_f, rel_bias, w_s, b_s, v_gain, branch_gain, w_out, final_g):
    depth = norm_g.shape[0]
    for l in range(depth):
        x = _layer(x, norm_g[l], w_in[l], b_f[l], rel_bias[l], w_s[l], b_s[l], v_gain[l],
                   branch_gain[l], w_out[l], final_g, last=(l == depth - 1))
    return x
```

```python
import functools

import jax
import jax.numpy as jnp
import numpy as np
from jax import lax
from jax.experimental import pallas as pl
from jax.experimental.pallas import tpu as pltpu

D_MODEL = 1024
N_HEADS = 4
HEAD_DIM = 64
D_BRANCH = N_HEADS * HEAD_DIM
CHUNK = 64
LOOKBACK_CHUNKS = 8
MAX_REL = 128
SG_CHUNK = 128
EPS = 1e-6
SCALE = HEAD_DIM ** -0.5
HEAD_PAD = 128
D_PAD = N_HEADS * HEAD_PAD
NEG = -1e30
LOG2E = 1.4426950408889634
SB_EXIT = 151.0
FOX_EXIT = 152.0
NORM_SLACK = 1.01
NORM_ROWS = 2048

F32 = jnp.float32
BF16 = jnp.bfloat16

TM_IN = 512
TQ_A = 256
LOOK_A = LOOKBACK_CHUNKS * CHUNK
NBLK_A = LOOK_A // TQ_A + 1
WIN_A = LOOK_A + TQ_A
ROLL_A = 1024
TQ_C = 1024
TK_C = 512
KPQ_C = TQ_C // TK_C
SUB_C = 2
TQ_D = 256
SUB_D = 2
TM_OUT = 512
VMEM_LIMIT = 56 * 1024 * 1024


def _nt_dot(a, b):
    return lax.dot_general(a, b, (((1,), (1,)), ((), ())), preferred_element_type=F32)


def _dot(a, b):
    return jnp.dot(a, b, preferred_element_type=F32)


def _bf16_part(x):
    return x.astype(BF16).astype(F32)


def _split3(x):
    hi = _bf16_part(x)
    mid = _bf16_part(x - hi)
    return hi, mid, x - hi - mid


def _inproj_kernel(x_ref, g_ref, wa_ref, wb_ref, wc_ref, wf_ref, wd_ref, bf_ref,
                   pa_ref, pb_ref, qc_ref, kc_ref, vc_ref, gc_ref, c_ref,
                   qd_ref, kd_ref, vd_ref, gd_ref, carry_ref):
    i = pl.program_id(1)
    tm = x_ref.shape[0]
    x = x_ref[...]
    ms = jnp.mean(x * x, axis=-1, keepdims=True)
    h = (x * lax.rsqrt(ms + EPS) * g_ref[...]).astype(BF16)

    pa = _dot(h, wa_ref[...])
    pa_ref[:, :D_BRANCH] = (pa[:, :D_BRANCH] * LOG2E).astype(BF16)
    pa_ref[:, D_BRANCH:] = pa[:, D_BRANCH:].astype(BF16)
    pb_ref[...] = _dot(h, wb_ref[...]).astype(BF16)

    f = _dot(h, wf_ref[...]) + bf_ref[...]
    ls = jnp.minimum(f, 0.0) - jnp.log(1.0 + jnp.exp(-jnp.abs(f)))
    row = lax.broadcasted_iota(jnp.int32, (tm, tm), 0)
    col = lax.broadcasted_iota(jnp.int32, (tm, tm), 1)
    tri = (row >= col).astype(BF16)
    ls_hi, ls_mid, ls_lo = _split3(ls)
    parts = _dot(tri, jnp.concatenate([ls_hi, ls_mid, ls_lo], axis=1).astype(BF16))
    local = parts[:, :HEAD_PAD] + parts[:, HEAD_PAD:2 * HEAD_PAD] + parts[:, 2 * HEAD_PAD:]

    @pl.when(i == 0)
    def _():
        carry_ref[...] = jnp.zeros_like(carry_ref)

    c = local + carry_ref[...]
    carry_ref[...] = c[tm - 1:tm, :]
    c2 = c * LOG2E
    c_ref[...] = c2.T

    lane = lax.broadcasted_iota(jnp.int32, (1, HEAD_PAD), 1)
    a0 = HEAD_DIM
    k_const = ((lane >= a0) & (lane < a0 + 3)).astype(F32)
    q_const = -((lane >= a0 + 3) & (lane < a0 + 6)).astype(F32)
    v_const = (lane == a0).astype(F32)

    def head_tile(p, part, hh, spare, scale=None, transpose=False):
        lo = part * D_BRANCH + (hh // 2) * HEAD_PAD
        tile = p[:, lo:lo + HEAD_PAD]
        if hh % 2:
            tile = pltpu.roll(tile, HEAD_DIM, 1)
        if scale is not None:
            tile = tile * scale
        tile = jnp.where(lane < HEAD_DIM, tile, spare)
        return (tile.T if transpose else tile).astype(BF16)

    pc = _dot(h, wc_ref[...])
    pieces = _split3(c2)
    for hh in range(N_HEADS):
        hi, mid, lo = (jnp.broadcast_to(p[:, hh:hh + 1], (tm, HEAD_PAD)) for p in pieces)
        q_spare = jnp.where(lane == a0, hi, jnp.where(lane == a0 + 1, mid, jnp.where(lane == a0 + 2, lo, q_const)))
        k_spare = jnp.where(lane == a0 + 3, hi, jnp.where(lane == a0 + 4, mid, jnp.where(lane == a0 + 5, lo, k_const)))
        qc_ref[hh] = head_tile(pc, 0, hh, q_spare, LOG2E)
        kc_ref[hh] = head_tile(pc, 1, hh, k_spare)
        vc_ref[hh] = head_tile(pc, 2, hh, v_const, transpose=True)
    gc_ref[...] = pc[:, 3 * D_BRANCH:4 * D_BRANCH].astype(BF16)

    pd = _dot(h, wd_ref[...])
    for hh in range(N_HEADS):
        qd_ref[hh] = head_tile(pd, 0, hh, 0.0, LOG2E)
        kd_ref[hh] = head_tile(pd, 1, hh, 0.0)
        vd_ref[hh] = head_tile(pd, 2, hh, 0.0)
    gd_ref[...] = pd[:, 3 * D_BRANCH:4 * D_BRANCH].astype(BF16)


def _inproj(x, norm_g, wa, wb, wc, wf, wd, bf):
    b, s, d = x.shape
    tm = TM_IN
    const = lambda shape: pl.BlockSpec(shape, lambda bi, i: (0,) * len(shape))
    tok = lambda n: pl.BlockSpec((None, tm, n), lambda bi, i: (bi, i, 0))
    head = pl.BlockSpec((None, N_HEADS, tm, HEAD_PAD), lambda bi, i: (bi, 0, i, 0))
    head_t = pl.BlockSpec((None, N_HEADS, HEAD_PAD, tm), lambda bi, i: (bi, 0, 0, i))
    head_shape = jax.ShapeDtypeStruct((b, N_HEADS, s, HEAD_PAD), BF16)
    return pl.pallas_call(
        _inproj_kernel,
        grid=(b, s // tm),
        in_specs=[tok(d), const((1, d)), const(wa.shape), const(wb.shape), const(wc.shape),
                  const(wf.shape), const(wd.shape), const((1, HEAD_PAD))],
        out_specs=[tok(wa.shape[1]), tok(wb.shape[1]), head, head, head_t, tok(D_BRANCH),
                   pl.BlockSpec((None, HEAD_PAD, tm), lambda bi, i: (bi, 0, i)),
                   head, head, head, tok(D_BRANCH)],
        out_shape=[jax.ShapeDtypeStruct((b, s, wa.shape[1]), BF16),
                   jax.ShapeDtypeStruct((b, s, wb.shape[1]), BF16),
                   head_shape, head_shape,
                   jax.ShapeDtypeStruct((b, N_HEADS, HEAD_PAD, s), BF16),
                   jax.ShapeDtypeStruct((b, s, D_BRANCH), BF16),
                   jax.ShapeDtypeStruct((b, HEAD_PAD, s), F32),
                   head_shape, head_shape, head_shape,
                   jax.ShapeDtypeStruct((b, s, D_BRANCH), BF16)],
        scratch_shapes=[pltpu.VMEM((1, HEAD_PAD), F32)],
        compiler_params=pltpu.CompilerParams(
            dimension_semantics=("arbitrary", "arbitrary"), vmem_limit_bytes=VMEM_LIMIT),
        name="inproj",
    )(x, norm_g, wa, wb, wc, wf, wd, bf)


def _chunk_attn_kernel(q_ref, *refs):
    k_refs, v_refs = refs[:NBLK_A], refs[NBLK_A:2 * NBLK_A]
    rb_ref, o_ref, bias_ref = refs[2 * NBLK_A:]
    i = pl.program_id(1)
    tq = q_ref.shape[0]

    @pl.when((pl.program_id(0) == 0) & (i == 0))
    def _():
        ti = lax.broadcasted_iota(jnp.int32, (tq, WIN_A), 0) // CHUNK
        sj = lax.broadcasted_iota(jnp.int32, (tq, WIN_A), 1) // CHUNK
        band = (sj >= ti) & (sj <= ti + LOOKBACK_CHUNKS)
        for hh in range(N_HEADS):
            rows = jnp.broadcast_to(rb_ref[hh:hh + 1, :], (tq, ROLL_A))
            toep = pltpu.roll(rows, 0, 1, stride=1, stride_axis=0)
            bias_ref[hh] = jnp.where(band, toep[:, :WIN_A], NEG)

    q = q_ref[...]
    k = jnp.concatenate([r[...] for r in k_refs], axis=0)
    v = jnp.concatenate([r[...] for r in v_refs], axis=0)
    head_of_lane = lax.broadcasted_iota(jnp.int32, (1, D_BRANCH), 1) // HEAD_DIM
    col = lax.broadcasted_iota(jnp.int32, (tq, WIN_A), 1)
    in_seq = col >= LOOK_A - i * tq
    out = jnp.zeros((tq, D_BRANCH), F32)
    for hh in range(N_HEADS):
        sel = head_of_lane == hh
        qh = jnp.where(sel, q, jnp.zeros_like(q))
        s = _nt_dot(qh, k) + bias_ref[hh]
        s = jnp.where(in_seq, s, NEG)
        m = jnp.max(s, axis=-1, keepdims=True)
        p = jnp.exp2(s - m)
        l = jnp.sum(p, axis=-1, keepdims=True)
        pv = _dot(p.astype(BF16), v)
        out = out + jnp.where(sel, pv * (1.0 / l), 0.0)
    o_ref[...] = out.astype(o_ref.dtype)


def _chunk_attn(pa, rb_row):
    b, s, _ = pa.shape
    tq = TQ_A
    blk = lambda colblk, back: pl.BlockSpec(
        (None, tq, D_BRANCH), lambda bi, i: (bi, jnp.maximum(i - back, 0), colblk))
    return pl.pallas_call(
        _chunk_attn_kernel,
        grid=(b, s // tq),
        in_specs=([blk(0, 0)] + [blk(1, back) for back in reversed(range(NBLK_A))]
                  + [blk(2, back) for back in reversed(range(NBLK_A))]
                  + [pl.BlockSpec(rb_row.shape, lambda bi, i: (0, 0))]),
        out_specs=pl.BlockSpec((None, tq, D_BRANCH), lambda bi, i: (bi, i, 0)),
        out_shape=jax.ShapeDtypeStruct((b, s, D_BRANCH), BF16),
        scratch_shapes=[pltpu.VMEM((N_HEADS, tq, WIN_A), F32)],
        compiler_params=pltpu.CompilerParams(
            dimension_semantics=("arbitrary", "arbitrary"), vmem_limit_bytes=VMEM_LIMIT),
        name="chunk_attn",
    )(*([pa] * (1 + 2 * NBLK_A)), rb_row)


def _chunk_bias_row(rel_bias):
    u = np.arange(ROLL_A)
    e = np.where(u < WIN_A, u, u - ROLL_A)
    rel = np.clip(LOOK_A - e, -MAX_REL, MAX_REL) + MAX_REL
    return rel_bias[:, rel].astype(F32) * LOG2E


def _fox_kernel(cend_ref, q_ref, k_ref, vt_ref, cq_ref, o_ref, *scratch):
    kmax_ref = scratch[-1]
    lane = lax.broadcasted_iota(jnp.int32, (1, HEAD_PAD), 1)

    @pl.when(pl.program_id(2) == 0)
    def _():
        def blk(r, mx):
            kf = k_ref[pl.ds(pl.multiple_of(r * NORM_ROWS, NORM_ROWS), NORM_ROWS), :].astype(F32)
            k2 = jnp.sum(jnp.where(lane < HEAD_DIM, kf * kf, 0.0), axis=1, keepdims=True)
            return jnp.maximum(mx, jnp.max(k2))
        kmax_ref[0] = jnp.sqrt(lax.fori_loop(0, k_ref.shape[0] // NORM_ROWS, blk, jnp.float32(0.0)))

    for sub in range(SUB_C):
        _fox_tile(pl.program_id(2) * SUB_C + sub, sub * TQ_C, cend_ref,
                  q_ref, k_ref, vt_ref, cq_ref, o_ref, *scratch)


def _fox_tile(i, q0, cend_ref, q_ref, k_ref, vt_ref, cq_ref, o_ref,
              sa_ref, sb_ref, sc_ref, sd_ref, ma_ref, mb_ref, mc_ref, md_ref, m_ref, acc_ref, kmax_ref):
    bi, hi = pl.program_id(0), pl.program_id(1)
    tq = TQ_C
    tk = sa_ref.shape[0]
    s_len = k_ref.shape[0]
    q = q_ref[q0:q0 + tq, :]
    m_ref[...] = jnp.full_like(m_ref, NEG)
    acc_ref[...] = jnp.zeros_like(acc_ref)
    lane = lax.broadcasted_iota(jnp.int32, (1, HEAD_PAD), 1)

    qf = q.astype(F32)
    q2 = jnp.where(lane < HEAD_DIM, qf * qf, 0.0).astype(BF16)
    n2 = _nt_dot(jnp.ones((8, HEAD_PAD), BF16), q2)[0:1, :]
    reach = jnp.sqrt(n2) * (kmax_ref[0] * NORM_SLACK) + cq_ref[:, q0:q0 + tq]
    cbase = (bi * pl.num_programs(1) + hi) * (s_len // tk)

    def scores(j, s_ref, cm_ref, diag_off=None):
        ks = pl.multiple_of(jnp.maximum(j, 0) * tk, tk)
        st = _nt_dot(k_ref[pl.ds(ks, tk), :], q)
        if diag_off is not None:
            kp = lax.broadcasted_iota(jnp.int32, (tk, tq), 0) + diag_off
            qp = lax.broadcasted_iota(jnp.int32, (tk, tq), 1)
            st = jnp.where(kp <= qp, st, NEG)
        s_ref[...] = st
        cm_ref[...] = jnp.max(st, axis=0, keepdims=True)

    def consume(j, s_ref, cm_ref):
        ks = pl.multiple_of(j * tk, tk)
        vt = vt_ref[:, pl.ds(ks, tk)]
        m_prev = m_ref[...]
        m_new = jnp.maximum(m_prev, cm_ref[...])
        alpha = jnp.exp2(m_prev - m_new)
        pt = jnp.exp2(s_ref[...] - m_new).astype(BF16)
        acc_ref[...] = alpha * acc_ref[...] + _dot(vt, pt)
        m_ref[...] = m_new

    top = KPQ_C * i + 1
    late = tq - tk
    ks_top = pl.multiple_of(top * tk, tk)
    st_top = _nt_dot(k_ref[pl.ds(ks_top, tk), :], q[late:])
    kp = lax.broadcasted_iota(jnp.int32, (tk, tk), 0)
    qp = lax.broadcasted_iota(jnp.int32, (tk, tk), 1)
    st_top = jnp.where(kp <= qp, st_top, NEG)
    cm_top = jnp.max(st_top, axis=0, keepdims=True)
    pt_top = jnp.exp2(st_top - cm_top).astype(BF16)
    set_x = ((sa_ref, ma_ref), (sb_ref, mb_ref))
    set_y = ((sc_ref, mc_ref), (sd_ref, md_ref))
    scores(top - 1, *set_y[0], 0)
    acc_ref[:, late:] = _dot(vt_ref[:, pl.ds(ks_top, tk)], pt_top)
    m_ref[:, late:] = cm_top
    scores(top - 2, *set_x[0])
    scores(top - 3, *set_x[1])
    consume(top - 1, *set_y[0])

    def cond(state):
        p, g = state
        return (p <= i) & (g - cend_ref[cbase + jnp.maximum(top - 2 * p, 0)] >= -FOX_EXIT)

    def pair_from(cur, nxt, p):
        j = top - 2 * p
        kmax_ref[1] = jnp.max(reach - m_ref[...])
        scores(j - 2, *nxt[0])
        consume(j, *cur[0])
        scores(j - 3, *nxt[1])
        consume(j - 1, *cur[1])

    def pair(state):
        p, _ = state

        @pl.when(p % 2 == 1)
        def _():
            pair_from(set_x, set_y, p)

        @pl.when(p % 2 == 0)
        def _():
            pair_from(set_y, set_x, p)

        return p + 1, kmax_ref[1]

    lax.while_loop(cond, pair, (jnp.int32(1), jnp.max(reach - m_ref[...])))

    acc = acc_ref[...]
    out_t = acc * (1.0 / acc[HEAD_DIM:HEAD_DIM + 1, :])
    d_idx = lax.broadcasted_iota(jnp.int32, out_t.shape, 0)
    o_ref[q0:q0 + tq, :] = jnp.where(d_idx < HEAD_DIM, out_t, 0.0).T.astype(o_ref.dtype)


def _fox(q, k, vt, c_row):
    b, nh, s, _ = q.shape
    tq, tk = TQ_C, TK_C
    step = SUB_C * tq
    cend = c_row[:, :, 0, tk - 1::tk].reshape(-1)
    qblk = pl.BlockSpec((None, None, step, HEAD_PAD), lambda bi, hi, i, ce: (bi, hi, i, 0))
    full = pl.BlockSpec((None, None, s, HEAD_PAD), lambda bi, hi, i, ce: (bi, hi, 0, 0))
    full_t = pl.BlockSpec((None, None, HEAD_PAD, s), lambda bi, hi, i, ce: (bi, hi, 0, 0))
    crow = pl.BlockSpec((None, None, 1, step), lambda bi, hi, i, ce: (bi, hi, 0, i))
    return pl.pallas_call(
        _fox_kernel,
        grid_spec=pltpu.PrefetchScalarGridSpec(
            num_scalar_prefetch=1,
            grid=(b, nh, s // step),
            in_specs=[qblk, full, full_t, crow],
            out_specs=pl.BlockSpec((None, step, HEAD_PAD), lambda bi, hi, i, ce: (bi, i, hi)),
            scratch_shapes=([pltpu.VMEM((tk, tq), F32)] * 4 + [pltpu.VMEM((1, tq), F32)] * 4
                            + [pltpu.VMEM((1, tq), F32), pltpu.VMEM((HEAD_PAD, tq), F32),
                               pltpu.SMEM((2,), F32)])),
        out_shape=jax.ShapeDtypeStruct((b, s, D_PAD), BF16),
        compiler_params=pltpu.CompilerParams(
            dimension_semantics=("arbitrary", "arbitrary", "arbitrary"),
            vmem_limit_bytes=VMEM_LIMIT),
        name="fox",
    )(cend, q, k, vt, c_row)


def _sb_kernel(q_ref, k_ref, v_ref, o_ref, r_ref, acc_ref):
    for sub in range(SUB_D):
        _sb_tile(pl.program_id(1) * SUB_D + sub, sub * TQ_D, q_ref, k_ref, v_ref, o_ref, r_ref, acc_ref)


def _sb_tile(i, q0, q_ref, k_ref, v_ref, o_ref, r_ref, acc_ref):
    nh, t = q_ref.shape[0], TQ_D
    row = lax.broadcasted_iota(jnp.int32, (t, t), 0)
    col = lax.broadcasted_iota(jnp.int32, (t, t), 1)
    tri = (row >= col).astype(BF16)
    tri2 = jnp.concatenate([tri, tri], axis=0)

    def tile(hh, kb, keep):
        ks = pl.multiple_of(kb * t, t)
        k = k_ref[hh, pl.ds(ks, t), :]
        v = v_ref[hh, pl.ds(ks, t), :]
        z = _nt_dot(q_ref[hh, q0:q0 + t, :], k)
        sp = jnp.maximum(z, 0.0) + jnp.log2(1.0 + jnp.exp2(-jnp.abs(z)))
        if keep is not None:
            sp = jnp.where(keep, sp, 0.0)
        hi = sp.astype(BF16)
        lo = (sp - hi.astype(F32)).astype(BF16)
        suf = _dot(jnp.concatenate([hi, lo], axis=1), tri2)
        return z, suf, v

    keep_a = col < row
    rmin = None
    for hh in range(nh):
        z_a, suf_a, v_a = tile(hh, i, keep_a)
        z_b, suf_b, v_b = tile(hh, jnp.maximum(i - 1, 0), None)
        v_b = jnp.where(i > 0, v_b, jnp.zeros_like(v_b))
        tot_a = suf_a[:, 0:1]
        a_a = jnp.where(keep_a, jnp.exp2(z_a - suf_a), 0.0)
        a_b = jnp.exp2(z_b - suf_b - tot_a)
        acc_ref[hh] = _dot(a_a.astype(BF16), v_a) + _dot(a_b.astype(BF16), v_b)
        r0 = tot_a + suf_b[:, 0:1]
        r_ref[hh] = r0
        rmin = jnp.min(r0) if rmin is None else jnp.minimum(rmin, jnp.min(r0))

    def cond(state):
        kb, rm = state
        return (kb >= 0) & (rm < SB_EXIT)

    def body(state):
        kb, _ = state
        rm = None
        for hh in range(nh):
            z, suf, v = tile(hh, kb, None)
            r = r_ref[hh]
            a = jnp.exp2(z - suf - r)
            acc_ref[hh] += _dot(a.astype(BF16), v)
            r = r + suf[:, 0:1]
            r_ref[hh] = r
            rm = jnp.min(r) if rm is None else jnp.minimum(rm, jnp.min(r))
        return kb - 1, rm

    lax.while_loop(cond, body, (i - 2, rmin))
    for hh in range(nh):
        o_ref[q0:q0 + t, hh * HEAD_PAD:(hh + 1) * HEAD_PAD] = acc_ref[hh].astype(o_ref.dtype)


def _stick_breaking(q, k, v):
    b, nh, s, _ = q.shape
    tq = SUB_D * TQ_D
    qblk = pl.BlockSpec((None, nh, tq, HEAD_PAD), lambda bi, i: (bi, 0, i, 0))
    full = pl.BlockSpec((None, nh, s, HEAD_PAD), lambda bi, i: (bi, 0, 0, 0),
                        pipeline_mode=pl.Buffered(1))
    return pl.pallas_call(
        _sb_kernel,
        grid=(b, s // tq),
        in_specs=[qblk, full, full],
        out_specs=pl.BlockSpec((None, tq, D_PAD), lambda bi, i: (bi, i, 0)),
        out_shape=jax.ShapeDtypeStruct((b, s, D_PAD), BF16),
        scratch_shapes=[pltpu.VMEM((nh, TQ_D, 1), F32), pltpu.VMEM((nh, TQ_D, HEAD_PAD), F32)],
        compiler_params=pltpu.CompilerParams(
            dimension_semantics=("arbitrary", "arbitrary"),
            vmem_limit_bytes=VMEM_LIMIT),
        name="stick_breaking",
    )(q, k, v)


def _out_kernel(x_ref, ya_ref, ga_ref, pb_ref, yc_ref, gc_ref, yd_ref, gd_ref,
                bga_ref, bgb_ref, bgc_ref, bgd_ref, vg_ref, ws_ref, bs_ref,
                wo_ref, fg_ref, o_ref, *, last):
    tm = x_ref.shape[0]

    def gated(y, g_ref, gain_ref):
        ms = jnp.sum(y * y, axis=-1, keepdims=True) * (1.0 / D_BRANCH)
        g = g_ref[...].astype(F32)
        silu = g * (1.0 / (1.0 + jnp.exp(-g)))
        return (y * lax.rsqrt(ms + EPS) * gain_ref[...] * silu).astype(BF16)

    u = pb_ref[:, 0:D_BRANCH].astype(F32)
    vb = pb_ref[:, D_BRANCH:2 * D_BRANCH].astype(F32)
    mu = jnp.mean(vb, axis=-1, keepdims=True)
    xc = vb - mu
    var = jnp.mean(xc * xc, axis=-1, keepdims=True)
    vn = (xc * lax.rsqrt(var + EPS) * vg_ref[...]).astype(BF16)
    row = lax.broadcasted_iota(jnp.int32, (SG_CHUNK, SG_CHUNK), 0)
    col = lax.broadcasted_iota(jnp.int32, (SG_CHUNK, SG_CHUNK), 1)
    group_of_lane = lax.broadcasted_iota(jnp.int32, (1, D_BRANCH), 1) // HEAD_DIM
    ws = [jnp.where(row >= col, ws_ref[g], 0.0).astype(BF16) for g in range(N_HEADS)]
    mixed = []
    for ci in range(tm // SG_CHUNK):
        vchunk = vn[ci * SG_CHUNK:(ci + 1) * SG_CHUNK]
        acc = bs_ref[...]
        for g in range(N_HEADS):
            acc = acc + jnp.where(group_of_lane == g, _dot(ws[g], vchunk), 0.0)
        mixed.append(acc)
    yb = u * jnp.concatenate(mixed, axis=0)

    def wo(j):
        return wo_ref[j * D_BRANCH:(j + 1) * D_BRANCH, :]

    out = x_ref[...]
    out = out + _dot(gated(ya_ref[...].astype(F32), ga_ref, bga_ref), wo(0))
    out = out + _dot(gated(yb, pb_ref.at[:, 2 * D_BRANCH:3 * D_BRANCH], bgb_ref), wo(1))
    lane = lax.broadcasted_iota(jnp.int32, (1, HEAD_PAD), 1)

    def unpad(y_ref):
        halves = []
        for pr in range(N_HEADS // 2):
            even = y_ref[:, (2 * pr) * HEAD_PAD:(2 * pr + 1) * HEAD_PAD].astype(F32)
            odd = y_ref[:, (2 * pr + 1) * HEAD_PAD:(2 * pr + 2) * HEAD_PAD].astype(F32)
            halves.append(jnp.where(lane < HEAD_DIM, even, pltpu.roll(odd, HEAD_DIM, 1)))
        return jnp.concatenate(halves, axis=1)

    out = out + _dot(gated(unpad(yc_ref), gc_ref, bgc_ref), wo(2))
    out = out + _dot(gated(unpad(yd_ref), gd_ref, bgd_ref), wo(3))
    if last:
        ms = jnp.mean(out * out, axis=-1, keepdims=True)
        out = out * lax.rsqrt(ms + EPS) * fg_ref[...]
    o_ref[...] = out


def _out_proj(x, ya, pa, pb, yc, gc, yd, gd, bga, bgb, bgc, bgd, vg, ws, bs,
              wo, fg, last):
    b, s, d = x.shape
    tm = TM_OUT
    tok = lambda n, colblk=0: pl.BlockSpec((None, tm, n), lambda bi, i: (bi, i, colblk))
    const = lambda a: pl.BlockSpec(a.shape, lambda bi, i: (0,) * a.ndim)
    return pl.pallas_call(
        functools.partial(_out_kernel, last=last),
        grid=(b, s // tm),
        in_specs=[tok(d), tok(D_BRANCH), tok(D_BRANCH, 3), tok(3 * D_BRANCH), tok(D_PAD), tok(D_BRANCH),
                  tok(D_PAD), tok(D_BRANCH), const(bga), const(bgb), const(bgc), const(bgd), const(vg),
                  const(ws), const(bs), const(wo), const(fg)],
        out_specs=tok(d),
        out_shape=jax.ShapeDtypeStruct((b, s, d), F32),
        compiler_params=pltpu.CompilerParams(
            dimension_semantics=("arbitrary", "arbitrary"), vmem_limit_bytes=VMEM_LIMIT),
        name="out_proj",
    )(x, ya, pa, pb, yc, gc, yd, gd, bga, bgb, bgc, bgd, vg, ws, bs, wo, fg)


def _layer(x, norm_g, w_in, b_f, rel_bias, w_s, b_s, v_gain, branch_gain, w_out, final_g, last):
    db = D_BRANCH
    o = 0
    wa = w_in[:, o:o + 4 * db]; o += 4 * db
    wb = w_in[:, o:o + 3 * db]; o += 3 * db
    wc = w_in[:, o:o + 4 * db]; o += 4 * db
    wf = w_in[:, o:o + N_HEADS]; o += N_HEADS
    wd = w_in[:, o:o + 4 * db]

    def q_scaled(w):
        return jnp.concatenate([w[:, :db] * SCALE, w[:, db:]], axis=1).astype(BF16)

    wf_p = jnp.pad(wf, ((0, 0), (0, HEAD_PAD - N_HEADS))).astype(BF16)
    bf_p = jnp.pad(b_f, (0, HEAD_PAD - N_HEADS)).reshape(1, HEAD_PAD)

    pa, pb, qc, kc, vc, gc, c2, qd, kd, vd, gd = _inproj(
        x, norm_g.reshape(1, -1), q_scaled(wa), wb.astype(BF16), q_scaled(wc), wf_p, q_scaled(wd), bf_p)

    ya = _chunk_attn(pa, _chunk_bias_row(rel_bias))
    yc = _fox(qc, kc, vc, c2[:, :N_HEADS, None, :])
    yd = _stick_breaking(qd, kd, vd)

    bs_tile = jnp.repeat(jnp.transpose(b_s), HEAD_DIM, axis=1)
    return _out_proj(
        x, ya, pa, pb, yc, gc, yd, gd,
        branch_gain[0].reshape(1, -1), branch_gain[1].reshape(1, -1),
        branch_gain[2].reshape(1, -1), branch_gain[3].reshape(1, -1),
        v_gain.reshape(1, -1), w_s, bs_tile,
        w_out.astype(BF16), final_g.reshape(1, -1), last)


def kernel(x, norm_g, w_in, b_f, rel_bias, w_s, b_s, v_gain, branch_gain, w_out, final_g):
    depth = norm_g.shape[0]
    for l in range(depth):
        x = _layer(x, norm_g[l], w_in[l], b_f[l], rel_bias[l], w_s[l], b_s[l], v_gain[l],
                   branch_gain[l], w_out[l], final_g, last=(l == depth - 1))
    return x
```
